```python
import math
import jax, jax.numpy as jnp
from jax import lax
import numpy as np

D_MODEL = 1024
BATCH = 8
SEQ = 2048
DEPTH = 4

CHUNK = 64
N_MEM = 256
EXPAND = 2
D_MIX = EXPAND * D_MODEL
GROUP_W = D_MIX // 4

RET_HEADS = 4
RET_DV = GROUP_W // RET_HEADS
RET_DK = RET_DV // 2
ROPE_BASE = 10000.0
S5_CH = 16
S5_GROUPS = GROUP_W // S5_CH
S5_STATE = 64
S5_DT_MIN = 1e-3
S5_DT_MAX = 1e-1
ML_HEADS = 4
ML_DH = GROUP_W // ML_HEADS
ML_CONV = 4
XA_HEADS = 4
XA_DH = GROUP_W // XA_HEADS

LN_EPS = 1e-5
GN_EPS = 1e-6
ALPHA = (2 * DEPTH) ** 0.25
BETA = (8 * DEPTH) ** -0.25

IN_WIDTHS = (RET_HEADS * RET_DK, RET_HEADS * RET_DK, GROUP_W, GROUP_W,
             GROUP_W, GROUP_W,
             GROUP_W, GROUP_W, GROUP_W, GROUP_W, ML_HEADS, ML_HEADS,
             GROUP_W, GROUP_W)
D_IN = 2 * RET_HEADS * RET_DK + 10 * GROUP_W + 2 * ML_HEADS

kernel_name = "hybrid_ret_s5_mlstm_memxattn_deepnorm"


def layer_norm(x, g, b):
    xf = x.astype(jnp.float32)
    mu = jnp.mean(xf, -1, keepdims=True)
    var = jnp.mean(jnp.square(xf - mu), -1, keepdims=True)
    return ((xf - mu) * lax.rsqrt(var + LN_EPS) * g.astype(jnp.float32) + b.astype(jnp.float32)).astype(x.dtype)


def head_norm(h):
    mu = jnp.mean(h, -1, keepdims=True)
    var = jnp.mean(jnp.square(h - mu), -1, keepdims=True)
    return (h - mu) * lax.rsqrt(var + GN_EPS)


def to_chunks(t):
    b, l = t.shape[:2]
    t = t.reshape(b, l // CHUNK, CHUNK, *t.shape[2:])
    return jnp.swapaxes(jnp.moveaxis(t, 1, 0), 2, 3)


def from_chunks(t):
    t = jnp.moveaxis(jnp.swapaxes(t, 2, 3), 0, 1)
    b, nc, c = t.shape[:3]
    return t.reshape(b, nc * c, *t.shape[3:])


def rotary(t, positions):
    half = t.shape[-1] // 2
    inv = ROPE_BASE ** (-jnp.arange(half, dtype=jnp.float32) / half)
    ang = positions.astype(jnp.float32)[..., None] * inv
    cos = jnp.cos(ang)[:, :, None, :]
    sin = jnp.sin(ang)[:, :, None, :]
    t1, t2 = t[..., :half], t[..., half:]
    return jnp.concatenate([t1 * cos - t2 * sin, t1 * sin + t2 * cos], -1)


def causal_conv(x, w):
    k, c = w.shape
    return lax.conv_general_dilated(x, w[:, None, :].astype(x.dtype), window_strides=(1,),
                                    padding=[(k - 1, 0)], dimension_numbers=('NWC', 'WIO', 'NWC'),
                                    feature_group_count=c)


def retention(q, k, v, positions):
    bsz = q.shape[0]
    q = rotary(q, positions)
    k = rotary(k, positions) * RET_DK ** -0.5
    log_g = jnp.log1p(-jnp.exp2(-5.0 - jnp.arange(RET_HEADS, dtype=jnp.float32)))
    r = jnp.arange(CHUNK, dtype=jnp.float32)
    intra = jnp.exp(jnp.abs(r[:, None] - r[None, :]) * log_g[:, None, None])
    q_dec = jnp.exp((r + 1.0) * log_g[:, None])
    k_dec = jnp.exp((CHUNK - 1.0 - r) * log_g[:, None])
    c_dec = jnp.exp(CHUNK * log_g)

    def step(S, xs):
        qc, kc, vc = xs
        att = jnp.einsum('bhrd,bhsd->bhrs', qc, kc) * intra
        o = (jnp.einsum('bhrs,bhse->bhre', att, vc)
             + q_dec[:, :, None] * jnp.einsum('bhrd,bhde->bhre', qc, S))
        S = c_dec[:, None, None] * S + jnp.einsum('bhsd,bhse->bhde', kc * k_dec[:, :, None], vc)
        return S, o

    S0 = jnp.zeros((bsz, RET_HEADS, RET_DK, RET_DV), jnp.float32)
    _, o = lax.scan(step, S0, (to_chunks(q), to_chunks(k), to_chunks(v)))
    return head_norm(from_chunks(o))


def s5(u, a_re, a_im, log_dt, b_re, b_im, c_re, c_im, d, w_glu):
    f32 = jnp.float32
    a_re, a_im, b_re, b_im = (t.astype(f32) for t in (a_re, a_im, b_re, b_im))
    bsz, l, _ = u.shape
    ug = u.reshape(bsz, l, S5_GROUPS, S5_CH)
    dt = jnp.exp(log_dt.astype(f32))[:, None]
    mag = jnp.exp(a_re * dt)
    ab_re = mag * jnp.cos(a_im * dt)
    ab_im = mag * jnp.sin(a_im * dt)
    den = a_re * a_re + a_im * a_im
    p = ab_re - 1.0
    f_re = (p * a_re + ab_im * a_im) / den
    f_im = (ab_im * a_re - p * a_im) / den
    bb_re = f_re[..., None] * b_re - f_im[..., None] * b_im
    bb_im = f_re[..., None] * b_im + f_im[..., None] * b_re
    bu_re = jnp.einsum('blgc,gpc->blgp', ug, bb_re)
    bu_im = jnp.einsum('blgc,gpc->blgp', ug, bb_im)

    def combine(e1, e2):
        a1r, a1i, b1r, b1i = e1
        a2r, a2i, b2r, b2i = e2
        return (a2r * a1r - a2i * a1i, a2r * a1i + a2i * a1r,
                a2r * b1r - a2i * b1i + b2r, a2r * b1i + a2i * b1r + b2i)

    _, _, xr, xi = lax.associative_scan(
        combine, (jnp.broadcast_to(ab_re, bu_re.shape), jnp.broadcast_to(ab_im, bu_re.shape), bu_re, bu_im),
        axis=1)
    y = (jnp.einsum('blgp,gcp->blgc', xr, c_re.astype(f32))
         - jnp.einsum('blgp,gcp->blgc', xi, c_im.astype(f32)))
    y = y.reshape(bsz, l, GROUP_W) + d.astype(f32) * u
    y = jax.nn.gelu(y)
    return y * jax.nn.sigmoid(y @ w_glu.astype(f32))


def mlstm(q, k, v, ig, lf):
    bsz, _, nh, dh = q.shape
    k = k * dh ** -0.5
    causal = jnp.tril(jnp.ones((CHUNK, CHUNK), bool))

    def step(carry, xs):
        C, n, m = carry
        qc, kc, vc, ic, fc = xs
        b = jnp.cumsum(fc, -1)
        dmat = jnp.where(causal, b[..., :, None] - b[..., None, :] + ic[..., None, :], -jnp.inf)
        inter = b + m[..., None]
        m_r = jnp.maximum(jnp.max(dmat, -1), inter)
        w = jnp.exp(dmat - m_r[..., None])
        w_inter = jnp.exp(inter - m_r)
        s = jnp.einsum('bhrd,bhsd->bhrs', qc, kc) * w
        num = (jnp.einsum('bhrs,bhsd->bhrd', s, vc)
               + w_inter[..., None] * jnp.einsum('bhed,bhrd->bhre', C, qc))
        den = jnp.sum(s, -1) + w_inter * jnp.einsum('bhd,bhrd->bhr', n, qc)
        h = num / jnp.maximum(jnp.abs(den), jnp.exp(-m_r))[..., None]
        b_last = b[..., -1]
        g = b_last[..., None] - b + ic
        m_new = jnp.maximum(b_last + m, jnp.max(g, -1))
        wk = jnp.exp(g - m_new[..., None])
        dec = jnp.exp(b_last + m - m_new)
        C = dec[..., None, None] * C + jnp.einsum('bhse,bhsd->bhed', vc * wk[..., None], kc)
        n = dec[..., None] * n + jnp.einsum('bhs,bhsd->bhd', wk, kc)
        return (C, n, m_new), h

    init = (jnp.zeros((bsz, nh, dh, dh), jnp.float32), jnp.zeros((bsz, nh, dh), jnp.float32),
            jnp.zeros((bsz, nh), jnp.float32))
    _, h = lax.scan(step, init, (to_chunks(q), to_chunks(k), to_chunks(v), to_chunks(ig), to_chunks(lf)))
    return from_chunks(h)


def memory_attention(q, mk, mv):
    s = jnp.einsum('blhd,bmhd->bhlm', q, mk).astype(jnp.float32) * XA_DH ** -0.5
    p = jax.nn.softmax(s, -1)
    return jnp.einsum('bhlm,bmhd->blhd', p, mv.astype(jnp.float32))


def hybrid_layer(x, mem, positions, w_in, s5_a_re, s5_a_im, s5_log_dt, s5_b_re, s5_b_im, s5_c_re,
                 s5_c_im, s5_d, s5_w_glu, ml_conv_w, ml_wq, ml_wk, ml_b_ig, ml_b_fg, ml_gn, xa_w_kv,
                 w_out, ln_g, ln_b):
    f32 = jnp.float32
    bsz, l, _ = x.shape
    proj = x @ w_in
    (r_q, r_k, r_v, r_z, s_u, s_z, m_x, m_v, m_o, m_z, m_i, m_f, a_q, a_z) = jnp.split(
        proj, list(np.cumsum(IN_WIDTHS)[:-1]), axis=-1)

    def heads(t, nh):
        return t.reshape(t.shape[0], t.shape[1], nh, -1).astype(f32)

    ret = retention(heads(r_q, RET_HEADS), heads(r_k, RET_HEADS), heads(r_v, RET_HEADS), positions)
    ret = ret.reshape(bsz, l, GROUP_W) * jax.nn.silu(r_z.astype(f32))

    s5o = s5(s_u.astype(f32), s5_a_re, s5_a_im, s5_log_dt, s5_b_re, s5_b_im, s5_c_re, s5_c_im, s5_d, s5_w_glu)
    s5o = s5o * jax.nn.silu(s_z.astype(f32))

    xc = heads(jax.nn.silu(causal_conv(m_x, ml_conv_w)), ML_HEADS)
    q = jnp.einsum('blhd,hde->blhe', xc, ml_wq.astype(f32))
    k = jnp.einsum('blhd,hde->blhe', xc, ml_wk.astype(f32))
    ig = m_i.astype(f32) + ml_b_ig.astype(f32)
    lf = jax.nn.log_sigmoid(m_f.astype(f32) + ml_b_fg.astype(f32))
    h = mlstm(q, k, heads(m_v, ML_HEADS), ig, lf) * jax.nn.sigmoid(heads(m_o, ML_HEADS))
    h = head_norm(h) * ml_gn.astype(f32).reshape(ML_HEADS, ML_DH)
    ml = h.reshape(bsz, l, GROUP_W) * jax.nn.silu(m_z.astype(f32))

    mk, mv = jnp.split(mem @ xa_w_kv, 2, axis=-1)
    xa = memory_attention(heads(a_q, XA_HEADS), heads(mk, XA_HEADS), heads(mv, XA_HEADS))
    xa = xa.reshape(bsz, l, GROUP_W) * jax.nn.silu(a_z.astype(f32))

    y = jnp.concatenate([ret, s5o, ml, xa], -1).astype(x.dtype) @ w_out
    return layer_norm(ALPHA * x + y, ln_g, ln_b)


def setup_inputs(seed: int = 0) -> dict:
    key = jax.random.key(seed)
    ks = jax.random.split(key, 24)
    nrm = jax.random.normal
    f32 = jnp.float32
    x = nrm(ks[0], (BATCH, SEQ, D_MODEL), f32)
    mem = nrm(ks[1], (BATCH, N_MEM, D_MODEL), f32)
    offs = jax.random.randint(ks[2], (BATCH, 1), 0, 4096, dtype=jnp.int32)
    positions = (offs + jnp.arange(SEQ, dtype=jnp.int32)[None, :]).astype(jnp.int32)
    w_in = nrm(ks[3], (DEPTH, D_MODEL, D_IN), f32) * D_MODEL ** -0.5
    s5_a_re = -0.5 + 0.01 * nrm(ks[4], (DEPTH, S5_GROUPS, S5_STATE), f32)
    s5_a_im = (math.pi * jnp.arange(S5_STATE, dtype=f32))[None, None, :] + 0.01 * nrm(ks[5], (DEPTH, S5_GROUPS, S5_STATE), f32)
    s5_log_dt = jax.random.uniform(ks[6], (DEPTH, S5_GROUPS), f32, math.log(S5_DT_MIN), math.log(S5_DT_MAX))
    s5_b_re = nrm(ks[7], (DEPTH, S5_GROUPS, S5_STATE, S5_CH), f32) * (2 * S5_CH) ** -0.5
    s5_b_im = nrm(ks[8], (DEPTH, S5_GROUPS, S5_STATE, S5_CH), f32) * (2 * S5_CH) ** -0.5
    s5_c_re = nrm(ks[9], (DEPTH, S5_GROUPS, S5_CH, S5_STATE), f32) * (2 * S5_STATE) ** -0.5
    s5_c_im = nrm(ks[10], (DEPTH, S5_GROUPS, S5_CH, S5_STATE), f32) * (2 * S5_STATE) ** -0.5
    s5_d = nrm(ks[11], (DEPTH, GROUP_W), f32)
    s5_w_glu = nrm(ks[12], (DEPTH, GROUP_W, GROUP_W), f32) * GROUP_W ** -0.5
    ml_conv_w = nrm(ks[13], (DEPTH, ML_CONV, GROUP_W), f32) * ML_CONV ** -0.5
    ml_wq = nrm(ks[14], (DEPTH, ML_HEADS, ML_DH, ML_DH), f32) * ML_DH ** -0.5
    ml_wk = nrm(ks[15], (DEPTH, ML_HEADS, ML_DH, ML_DH), f32) * ML_DH ** -0.5
    ml_b_ig = 0.1 * nrm(ks[16], (DEPTH, ML_HEADS), f32)
    ml_b_fg = jnp.linspace(3.0, 6.0, ML_HEADS, dtype=f32)[None, :] + 0.1 * nrm(ks[17], (DEPTH, ML_HEADS), f32)
    ml_gn = 1.0 + 0.02 * nrm(ks[18], (DEPTH, GROUP_W), f32)
    xa_w_kv = nrm(ks[19], (DEPTH, D_MODEL, 2 * GROUP_W), f32) * D_MODEL ** -0.5
    w_out = nrm(ks[20], (DEPTH, D_MIX, D_MODEL), f32) * (D_MIX ** -0.5 * BETA)
    ln_g = 1.0 + 0.02 * nrm(ks[21], (DEPTH, D_MODEL), f32)
    ln_b = 0.02 * nrm(ks[22], (DEPTH, D_MODEL), f32)
    return {"x": x, "mem": mem, "positions": positions, "w_in": w_in,
            "s5_a_re": s5_a_re, "s5_a_im": s5_a_im, "s5_log_dt": s5_log_dt,
            "s5_b_re": s5_b_re, "s5_b_im": s5_b_im, "s5_c_re": s5_c_re, "s5_c_im": s5_c_im,
            "s5_d": s5_d, "s5_w_glu": s5_w_glu, "ml_conv_w": ml_conv_w, "ml_wq": ml_wq,
            "ml_wk": ml_wk, "ml_b_ig": ml_b_ig, "ml_b_fg": ml_b_fg, "ml_gn": ml_gn,
            "xa_w_kv": xa_w_kv, "w_out": w_out, "ln_g": ln_g, "ln_b": ln_b}


def reference(x, mem, positions, w_in, s5_a_re, s5_a_im, s5_log_dt, s5_b_re, s5_b_im, s5_c_re,
              s5_c_im, s5_d, s5_w_glu, ml_conv_w, ml_wq, ml_wk, ml_b_ig, ml_b_fg, ml_gn, xa_w_kv,
              w_out, ln_g, ln_b):
    for l in range(DEPTH):
        x = hybrid_layer(x, mem, positions, w_in[l], s5_a_re[l], s5_a_im[l], s5_log_dt[l],
                         s5_b_re[l], s5_b_im[l], s5_c_re[l], s5_c_im[l], s5_d[l], s5_w_glu[l],
                         ml_conv_w[l], ml_wq[l], ml_wk[l], ml_b_ig[l], ml_b_fg[l], ml_gn[l],
                         xa_w_kv[l], w_out[l], ln_g[l], ln_b[l])
    return x
```

```python
import functools
import math

import jax
import jax.numpy as jnp
from jax import lax
from jax.experimental import pallas as pl
from jax.experimental.pallas import tpu as pltpu

F32 = jnp.float32
BF16 = jnp.bfloat16

D_MODEL = 1024
CHUNK = 64
N_MEM = 256
GROUP_W = 512
HEADS = 4
RET_DK = 64
RET_DV = 128
ROPE_BASE = 10000.0
S5_CH = 16
S5_GROUPS = 32
S5_STATE = 64
S5_LANES = S5_GROUPS * S5_STATE
ML_DH = 128
ML_CONV = 4
XA_DH = 128
LN_EPS = 1e-5
GN_EPS = 1e-6
DEPTH = 4
ALPHA = (2 * DEPTH) ** 0.25

SUBLANES = 8
VMEM_LIMIT = 56 * 1024 * 1024

W_ML, W_S5, W_XA, W_RET, W_IF = 4 * GROUP_W, 2 * GROUP_W, 2 * GROUP_W, 3 * GROUP_W, 128


def _params(sem):
    return pltpu.CompilerParams(dimension_semantics=sem, vmem_limit_bytes=VMEM_LIMIT)


def _sigmoid(x):
    return 1.0 / (1.0 + jnp.exp(-x))


def _silu(x):
    return x * _sigmoid(x)


def _norm_rows(t, eps):
    mu = jnp.mean(t, axis=-1, keepdims=True)
    d = t - mu
    var = jnp.mean(d * d, axis=-1, keepdims=True)
    return d * lax.rsqrt(var + eps)


def _dot(a, b):
    return jnp.dot(a, b, preferred_element_type=F32)


def _dot_nt(a, b):
    return lax.dot_general(a, b, (((1,), (1,)), ((), ())), preferred_element_type=F32)


def _dot_tn(a, b):
    return lax.dot_general(a, b, (((0,), (0,)), ((), ())), preferred_element_type=F32)


def _rope_kernel(pos_ref, inv_ref, cos_ref, sin_ref):
    ang = pos_ref[...].astype(F32) * inv_ref[...]
    cos_ref[...] = jnp.cos(ang)
    sin_ref[...] = jnp.sin(ang)


def _rope_tables(positions):
    m = positions.size
    half = RET_DK // 2
    inv = ROPE_BASE ** (-jnp.arange(half, dtype=F32) / half)
    inv = jnp.tile(inv, HEADS)[None, :]
    tm = min(512, m)
    return pl.pallas_call(
        _rope_kernel,
        grid=(m // tm,),
        in_specs=[pl.BlockSpec((tm, 1), lambda i: (i, 0)),
                  pl.BlockSpec((1, 128), lambda i: (0, 0))],
        out_specs=[pl.BlockSpec((tm, 128), lambda i: (i, 0))] * 2,
        out_shape=[jax.ShapeDtypeStruct((m, 128), F32)] * 2,
        compiler_params=_params(("parallel",)),
        name="rope_tables",
    )(positions.reshape(m, 1), inv)


def _proj_kernel(x_ref, w_ref, *o_refs):
    xb = x_ref[...].astype(BF16)
    off = 0
    for o in o_refs:
        if len(o.shape) == 3:
            for j in range(o.shape[0]):
                o[j] = _dot(xb, w_ref[:, off:off + 128])
                off += 128
        else:
            n = o.shape[1]
            o[...] = _dot(xb, w_ref[:, off:off + n])
            off += n


def _project(x2, w_in_r):
    m = x2.shape[0]
    tm = min(256, m)
    row = lambda i: (i, 0)
    specs = [pl.BlockSpec((tm, W_ML), row),
             pl.BlockSpec((W_S5 // 128, tm, 128), lambda i: (0, i, 0)),
             pl.BlockSpec((tm, W_XA), row),
             pl.BlockSpec((tm, W_RET), row),
             pl.BlockSpec((tm, W_IF), row)]
    shapes = [(m, W_ML), (W_S5 // 128, m, 128), (m, W_XA), (m, W_RET), (m, W_IF)]
    return pl.pallas_call(
        _proj_kernel,
        grid=(m // tm,),
        in_specs=[pl.BlockSpec((tm, D_MODEL), row),
                  pl.BlockSpec(w_in_r.shape, lambda i: (0, 0))],
        out_specs=specs,
        out_shape=[jax.ShapeDtypeStruct(s, F32) for s in shapes],
        compiler_params=_params(("parallel",)),
        name="in_proj",
    )(x2, w_in_r)


def _kv_kernel(m_ref, w_ref, k_ref, v_ref):
    mb = m_ref[...].astype(BF16)
    k_ref[...] = _dot(mb, w_ref[:, :GROUP_W]).astype(BF16)
    v_ref[...] = _dot(mb, w_ref[:, GROUP_W:]).astype(BF16)


def _memory_kv(mem2, w_kv):
    m = mem2.shape[0]
    return pl.pallas_call(
        _kv_kernel,
        grid=(m // N_MEM,),
        in_specs=[pl.BlockSpec((N_MEM, D_MODEL), lambda i: (i, 0)),
                  pl.BlockSpec(w_kv.shape, lambda i: (0, 0))],
        out_specs=[pl.BlockSpec((N_MEM, GROUP_W), lambda i: (i, 0))] * 2,
        out_shape=[jax.ShapeDtypeStruct((m, GROUP_W), BF16)] * 2,
        compiler_params=_params(("parallel",)),
        name="memory_kv",
    )(mem2, w_kv)


def _ret_kernel(p_ref, cos_ref, sin_ref, hmask_ref, intra_ref, qdec_ref, kdec_ref, cdec_ref,
                o_ref, s_ref):
    @pl.when(pl.program_id(1) == 0)
    def _():
        s_ref[...] = jnp.zeros_like(s_ref)

    cs = cos_ref[...]
    sn = sin_ref[...]

    def rot(t):
        t1 = t[:, :128]
        t2 = t[:, 128:]
        return jnp.concatenate([t1 * cs - t2 * sn, t1 * sn + t2 * cs], axis=1)

    qr = rot(p_ref[:, 0:256])
    kr = rot(p_ref[:, 256:512])
    vb = p_ref[:, 512:1024].astype(BF16)
    z = p_ref[:, 1024:1536]
    krb = kr.astype(BF16)
    state = s_ref[...]
    sb = state.astype(BF16)
    outs = []
    for h in range(HEADS):
        cols = slice(RET_DV * h, RET_DV * (h + 1))
        qm = (qr * hmask_ref[h:h + 1, :]).astype(BF16)
        att = (_dot_nt(qm, krb) * intra_ref[h]).astype(BF16)
        o = _dot(att, vb[:, cols]) + qdec_ref[:, cols] * _dot(qm, sb[:, cols])
        outs.append(_norm_rows(o, GN_EPS))
    o_ref[...] = jnp.concatenate(outs, axis=1) * _silu(z)
    kd = (kr * kdec_ref[...]).astype(BF16)
    s_ref[...] = cdec_ref[...] * state + _dot_tn(kd, vb)


def _retention(p_ret, cos, sin, tabs, bsz, seq):
    nc = seq // CHUNK
    row = lambda b, c: (b * nc + c, 0)
    const2 = lambda b, c: (0, 0)
    hmask, intra, qdec, kdec, cdec = tabs
    return pl.pallas_call(
        _ret_kernel,
        grid=(bsz, nc),
        in_specs=[pl.BlockSpec((CHUNK, W_RET), row),
                  pl.BlockSpec((CHUNK, 128), row),
                  pl.BlockSpec((CHUNK, 128), row),
                  pl.BlockSpec(hmask.shape, const2),
                  pl.BlockSpec(intra.shape, lambda b, c: (0, 0, 0)),
                  pl.BlockSpec(qdec.shape, const2),
                  pl.BlockSpec(kdec.shape, const2),
                  pl.BlockSpec(cdec.shape, const2)],
        out_specs=pl.BlockSpec((CHUNK, GROUP_W), row),
        out_shape=jax.ShapeDtypeStruct((bsz * seq, GROUP_W), F32),
        scratch_shapes=[pltpu.VMEM((HEADS * RET_DK, HEADS * RET_DV), F32)],
        compiler_params=_params(("parallel", "arbitrary")),
        name="retention",
    )(p_ret, cos, sin, hmask, intra, qdec, kdec, cdec)


def _retention_tables():
    log_g = jnp.log1p(-jnp.exp2(-5.0 - jnp.arange(HEADS, dtype=F32)))
    r = jnp.arange(CHUNK, dtype=F32)
    intra = jnp.exp(jnp.abs(r[:, None] - r[None, :]) * log_g[:, None, None])
    q_dec = jnp.exp((r + 1.0) * log_g[:, None])
    k_dec = jnp.exp((CHUNK - 1.0 - r) * log_g[:, None])
    c_dec = jnp.exp(CHUNK * log_g)
    lane_head = (jnp.arange(HEADS * RET_DK) % 128) // (RET_DK // 2)
    hmask = (lane_head[None, :] == jnp.arange(HEADS)[:, None]).astype(F32)
    qdec = jnp.repeat(q_dec.T, RET_DV, axis=1)
    kdec = k_dec.T[:, lane_head]
    cdec = jnp.repeat(c_dec, RET_DV)[None, :]
    return hmask, intra, qdec, kdec, cdec


def _gelu_tanh(x):
    return 0.5 * x * (1.0 + jnp.tanh(math.sqrt(2.0 / math.pi) * (x + 0.044715 * (x * x * x))))


S5_LANE_BLOCK = 512


def _s5_kernel(p_ref, bbd_ref, cbd_ref, a_ref, as_ref, d_ref, wglu_ref, o_ref, bu_ref, carry_ref,
               *, sub):
    @pl.when(pl.program_id(1) == 0)
    def _():
        carry_ref[...] = jnp.zeros_like(carry_ref)

    def gather(j):
        return jnp.concatenate([p_ref[j, pl.ds(s, SUBLANES, stride=sub), :] for s in range(sub)],
                               axis=0)

    nblk = GROUP_W // 128
    u = jnp.concatenate([gather(j) for j in range(nblk)], axis=1)
    z = jnp.concatenate([gather(nblk + j) for j in range(nblk)], axis=1)
    bu_ref[...] = _dot(u.astype(BF16), bbd_ref[...])

    ns = S5_LANES
    lb = S5_LANE_BLOCK
    for blk in range(ns // lb):
        re = slice(blk * lb, (blk + 1) * lb)
        im = slice(ns + blk * lb, ns + (blk + 1) * lb)
        ar = jnp.broadcast_to(a_ref[0:1, re], (SUBLANES, lb))
        ai = jnp.broadcast_to(a_ref[1:2, re], (SUBLANES, lb))

        def step(s, xr, xi):
            rows = pl.ds(pl.multiple_of(s * SUBLANES, SUBLANES), SUBLANES)
            nr = ar * xr - ai * xi + bu_ref[rows, re]
            ni = ar * xi + ai * xr + bu_ref[rows, im]
            return rows, nr, ni

        def end_state(s, carry):
            _, nr, ni = step(s, *carry)
            return nr, ni

        zero = jnp.zeros((SUBLANES, lb), F32)
        er, ei = lax.fori_loop(0, sub, end_state, (zero, zero), unroll=8)

        asr = as_ref[0:1, re]
        asi = as_ref[1:2, re]
        xr = carry_ref[0:1, re]
        xi = carry_ref[1:2, re]
        start_r, start_i = [], []
        for k in range(SUBLANES):
            start_r.append(xr)
            start_i.append(xi)
            xr, xi = (asr * xr - asi * xi + er[k:k + 1, :], asr * xi + asi * xr + ei[k:k + 1, :])
        carry_ref[0:1, re] = xr
        carry_ref[1:2, re] = xi

        def write_states(s, carry):
            rows, nr, ni = step(s, *carry)
            bu_ref[rows, re] = nr
            bu_ref[rows, im] = ni
            return nr, ni

        lax.fori_loop(0, sub, write_states,
                      (jnp.concatenate(start_r, axis=0), jnp.concatenate(start_i, axis=0)), unroll=8)

    y = _dot(bu_ref[...].astype(BF16), cbd_ref[...]) + d_ref[...] * u
    y = _gelu_tanh(y)
    y = y * _sigmoid(_dot(y.astype(BF16), wglu_ref[...]))
    res = y * _silu(z)
    for j in range(nblk):
        for s in range(sub):
            o_ref[j, pl.ds(s, SUBLANES, stride=sub), :] = (
                res[s * SUBLANES:(s + 1) * SUBLANES, 128 * j:128 * (j + 1)])


def _s5(p_s5, w, bsz, seq):
    bbd, cbd, a, a_sub, d, wglu, t_blk = w
    nb = seq // t_blk
    row = lambda b, j: (0, b * nb + j, 0)
    const2 = lambda b, j: (0, 0)
    return pl.pallas_call(
        functools.partial(_s5_kernel, sub=t_blk // SUBLANES),
        grid=(bsz, nb),
        in_specs=[pl.BlockSpec((W_S5 // 128, t_blk, 128), row)] +
                 [pl.BlockSpec(t.shape, const2) for t in (bbd, cbd, a, a_sub, d, wglu)],
        out_specs=pl.BlockSpec((GROUP_W // 128, t_blk, 128), row),
        out_shape=jax.ShapeDtypeStruct((GROUP_W // 128, bsz * seq, 128), F32),
        scratch_shapes=[pltpu.VMEM((t_blk, 2 * S5_LANES), F32),
                        pltpu.VMEM((2, S5_LANES), F32)],
        compiler_params=_params(("parallel", "arbitrary")),
        name="s5",
    )(p_s5, bbd, cbd, a, a_sub, d, wglu)


def _s5_block_tokens(seq):
    t = min(512, seq)
    assert seq % t == 0 and t % (SUBLANES * SUBLANES) == 0
    return t


def _s5_weights(a_re, a_im, log_dt, b_re, b_im, c_re, c_im, d, w_glu, t_blk):
    dt = jnp.exp(log_dt)[:, None]
    mag = jnp.exp(a_re * dt)
    ab_re = mag * jnp.cos(a_im * dt)
    ab_im = mag * jnp.sin(a_im * dt)
    den = a_re * a_re + a_im * a_im
    p = ab_re - 1.0
    f_re = (p * a_re + ab_im * a_im) / den
    f_im = (ab_im * a_re - p * a_im) / den
    bb_re = f_re[..., None] * b_re - f_im[..., None] * b_im
    bb_im = f_re[..., None] * b_im + f_im[..., None] * b_re
    eye = jnp.eye(S5_GROUPS, dtype=F32)

    def blockdiag_in(t):
        return (eye[:, None, :, None] * jnp.swapaxes(t, 1, 2)[:, :, None, :]).reshape(GROUP_W, S5_LANES)

    def blockdiag_out(t):
        return (eye[:, None, :, None] * jnp.swapaxes(t, 1, 2)[:, :, None, :]).reshape(S5_LANES, GROUP_W)

    bbd = jnp.concatenate([blockdiag_in(bb_re), blockdiag_in(bb_im)], axis=1).astype(BF16)
    cbd = jnp.concatenate([blockdiag_out(c_re), blockdiag_out(-c_im)], axis=0).astype(BF16)
    sr, si = ab_re, ab_im
    for _ in range(int(math.log2(t_blk // SUBLANES))):
        sr, si = sr * sr - si * si, 2.0 * sr * si
    a = jnp.stack([ab_re.reshape(-1), ab_im.reshape(-1)])
    a_sub = jnp.stack([sr.reshape(-1), si.reshape(-1)])
    return bbd, cbd, a, a_sub, d[None, :], w_glu.astype(BF16), t_blk


def _log_sigmoid(x):
    return jnp.minimum(x, 0.0) - jnp.log1p(jnp.exp(-jnp.abs(x)))


def _ml_kernel(p_ref, g_ref, conv_ref, wq_ref, wk_ref, bias_ref, gn_ref, o_ref,
               xbuf_ref, c_ref, n_ref, m_ref):
    @pl.when(pl.program_id(1) == 0)
    def _():
        xbuf_ref[0:SUBLANES, :] = jnp.zeros((SUBLANES, GROUP_W), F32)
        c_ref[...] = jnp.zeros_like(c_ref)
        n_ref[...] = jnp.zeros_like(n_ref)
        m_ref[...] = jnp.zeros_like(m_ref)

    xbuf_ref[SUBLANES:, :] = p_ref[:, 0:GROUP_W]
    conv = jnp.zeros((CHUNK, GROUP_W), F32)
    for j in range(ML_CONV):
        off = SUBLANES - (ML_CONV - 1) + j
        conv = conv + conv_ref[j:j + 1, :] * xbuf_ref[off:off + CHUNK, :]
    xbuf_ref[0:SUBLANES, :] = xbuf_ref[CHUNK:CHUNK + SUBLANES, :]
    xc = _silu(conv).astype(BF16)
    v = p_ref[:, GROUP_W:2 * GROUP_W]
    og = p_ref[:, 2 * GROUP_W:3 * GROUP_W]
    z = p_ref[:, 3 * GROUP_W:4 * GROUP_W]

    gates = g_ref[...] + bias_ref[...]
    lane = lax.broadcasted_iota(jnp.int32, (CHUNK, 128), 1)
    row = lax.broadcasted_iota(jnp.int32, (CHUNK, 128), 0)
    csum = _log_sigmoid(gates)
    sh = 1
    while sh < CHUNK:
        csum = csum + jnp.where(row >= sh, pltpu.roll(csum, sh, 0), 0.0)
        sh *= 2
    gb = jnp.where(lane < HEADS, gates, csum)
    gbt = jnp.concatenate([gb, jnp.zeros_like(gb)], axis=0).T
    r_i = lax.broadcasted_iota(jnp.int32, (CHUNK, CHUNK), 0)
    c_i = lax.broadcasted_iota(jnp.int32, (CHUNK, CHUNK), 1)
    causal = c_i <= r_i

    outs = []
    for h in range(HEADS):
        cols = slice(ML_DH * h, ML_DH * (h + 1))
        xh = xc[:, cols]
        q = _dot(xh, wq_ref[h])
        k = _dot(xh, wk_ref[h]) * ML_DH ** -0.5
        qb = q.astype(BF16)
        kb = k.astype(BF16)
        vh = v[:, cols]
        i_col = gb[:, h:h + 1]
        b_col = gb[:, HEADS + h:HEADS + h + 1]
        i_row = gbt[h:h + 1, 0:CHUNK]
        b_row = gbt[HEADS + h:HEADS + h + 1, 0:CHUNK]
        m_prev = m_ref[h:h + 1, 0:1]
        n_prev = n_ref[h:h + 1, :]
        ct_prev = c_ref[h]

        dmat = jnp.where(causal, b_col - b_row + i_row, -jnp.inf)
        inter = b_col + m_prev
        m_r = jnp.maximum(jnp.max(dmat, axis=1, keepdims=True), inter)
        w = jnp.exp(dmat - m_r)
        w_inter = jnp.exp(inter - m_r)
        s = _dot_nt(qb, kb) * w
        num = _dot(s.astype(BF16), vh.astype(BF16)) + w_inter * _dot(qb, ct_prev.astype(BF16))
        den = (jnp.sum(s, axis=1, keepdims=True)
               + w_inter * jnp.sum(q * n_prev, axis=1, keepdims=True))
        hh = num / jnp.maximum(jnp.abs(den), jnp.exp(-m_r))

        b_last = b_col[CHUNK - 1:CHUNK, :]
        g = b_last - b_col + i_col
        m_new = jnp.maximum(b_last + m_prev, jnp.max(g, axis=0, keepdims=True))
        wk = jnp.exp(g - m_new)
        dec = jnp.exp(b_last + m_prev - m_new)
        c_ref[h] = dec * ct_prev + _dot_tn(kb, (vh * wk).astype(BF16))
        n_ref[h:h + 1, :] = dec * n_prev + jnp.sum(wk * k, axis=0, keepdims=True)
        m_ref[h:h + 1, :] = jnp.broadcast_to(m_new, (1, 128))

        hh = hh * _sigmoid(og[:, cols])
        outs.append(_norm_rows(hh, GN_EPS))
    o_ref[...] = jnp.concatenate(outs, axis=1) * gn_ref[...] * _silu(z)


def _mlstm(p_ml, p_if, w, bsz, seq):
    conv_w, wq, wk, bias, gn = w
    nc = seq // CHUNK
    row = lambda b, c: (b * nc + c, 0)
    const2 = lambda b, c: (0, 0)
    const3 = lambda b, c: (0, 0, 0)
    return pl.pallas_call(
        _ml_kernel,
        grid=(bsz, nc),
        in_specs=[pl.BlockSpec((CHUNK, W_ML), row),
                  pl.BlockSpec((CHUNK, W_IF), row),
                  pl.BlockSpec(conv_w.shape, const2),
                  pl.BlockSpec(wq.shape, const3),
                  pl.BlockSpec(wk.shape, const3),
                  pl.BlockSpec(bias.shape, const2),
                  pl.BlockSpec(gn.shape, const2)],
        out_specs=pl.BlockSpec((CHUNK, GROUP_W), row),
        out_shape=jax.ShapeDtypeStruct((bsz * seq, GROUP_W), F32),
        scratch_shapes=[pltpu.VMEM((CHUNK + SUBLANES, GROUP_W), F32),
                        pltpu.VMEM((HEADS, ML_DH, ML_DH), F32),
                        pltpu.VMEM((SUBLANES, ML_DH), F32),
                        pltpu.VMEM((SUBLANES, 128), F32)],
        compiler_params=_params(("parallel", "arbitrary")),
        name="mlstm",
    )(p_ml, p_if, conv_w, wq, wk, bias, gn)


def _xa_kernel(p_ref, mk_ref, mv_ref, o_ref):
    z = p_ref[:, GROUP_W:]
    outs = []
    for h in range(HEADS):
        cols = slice(XA_DH * h, XA_DH * (h + 1))
        s = _dot_nt(p_ref[:, cols].astype(BF16), mk_ref[:, cols]) * XA_DH ** -0.5
        e = jnp.exp(s - jnp.max(s, axis=1, keepdims=True))
        p = e / jnp.sum(e, axis=1, keepdims=True)
        outs.append(_dot(p.astype(BF16), mv_ref[:, cols]))
    o_ref[...] = jnp.concatenate(outs, axis=1) * _silu(z)


def _cross_attention(p_xa, mk, mv, bsz, seq):
    t_blk = min(256, seq)
    nb = seq // t_blk
    row = lambda b, j: (b * nb + j, 0)
    mem = lambda b, j: (b, 0)
    return pl.pallas_call(
        _xa_kernel,
        grid=(bsz, nb),
        in_specs=[pl.BlockSpec((t_blk, W_XA), row),
                  pl.BlockSpec((N_MEM, GROUP_W), mem),
                  pl.BlockSpec((N_MEM, GROUP_W), mem)],
        out_specs=pl.BlockSpec((t_blk, GROUP_W), row),
        out_shape=jax.ShapeDtypeStruct((bsz * seq, GROUP_W), F32),
        compiler_params=_params(("parallel", "parallel")),
        name="memory_attention",
    )(p_xa, mk, mv)


def _out_kernel(x_ref, r_ref, s_ref, m_ref, a_ref, w_ref, g_ref, b_ref, o_ref):
    y = None
    s5o = jnp.concatenate([s_ref[j] for j in range(s_ref.shape[0])], axis=1)
    for i, t in enumerate((r_ref[...], s5o, m_ref[...], a_ref[...])):
        part = _dot(t.astype(BF16), w_ref[GROUP_W * i:GROUP_W * (i + 1), :])
        y = part if y is None else y + part
    t = ALPHA * x_ref[...] + y
    o_ref[...] = _norm_rows(t, LN_EPS) * g_ref[...] + b_ref[...]


def _out_project(x2, groups, w_out, ln_g, ln_b):
    m = x2.shape[0]
    tm = min(256, m)
    row = lambda i: (i, 0)
    const = lambda i: (0, 0)
    return pl.pallas_call(
        _out_kernel,
        grid=(m // tm,),
        in_specs=[pl.BlockSpec((tm, D_MODEL), row),
                  pl.BlockSpec((tm, GROUP_W), row),
                  pl.BlockSpec((GROUP_W // 128, tm, 128), lambda i: (0, i, 0)),
                  pl.BlockSpec((tm, GROUP_W), row),
                  pl.BlockSpec((tm, GROUP_W), row),
                  pl.BlockSpec(w_out.shape, const),
                  pl.BlockSpec((1, D_MODEL), const),
                  pl.BlockSpec((1, D_MODEL), const)],
        out_specs=pl.BlockSpec((tm, D_MODEL), row),
        out_shape=jax.ShapeDtypeStruct((m, D_MODEL), F32),
        compiler_params=_params(("parallel",)),
        name="out_proj_norm",
    )(x2, *groups, w_out, ln_g, ln_b)


def _reorder_w_in(w_in):
    o = 0
    cols = {}
    for name, w in (("rq", 256), ("rk", 256), ("rv", 512), ("rz", 512), ("su", 512), ("sz", 512),
                    ("mx", 512), ("mv", 512), ("mo", 512), ("mz", 512), ("mi", 4), ("mf", 4),
                    ("aq", 512), ("az", 512)):
        cols[name] = w_in[:, o:o + w]
        o += w
    half = RET_DK // 2
    perm = jnp.array([h * RET_DK + s * half + j for s in range(2) for h in range(HEADS)
                      for j in range(half)], jnp.int32)
    rq = cols["rq"][:, perm]
    rk = cols["rk"][:, perm] * RET_DK ** -0.5
    pad = jnp.zeros((w_in.shape[0], W_IF - 2 * HEADS), w_in.dtype)
    return jnp.concatenate(
        [cols["mx"], cols["mv"], cols["mo"], cols["mz"], cols["su"], cols["sz"],
         cols["aq"], cols["az"], rq, rk, cols["rv"], cols["rz"], cols["mi"], cols["mf"], pad],
        axis=1).astype(BF16)


def _layer(x2, mem2, cos, sin, ret_tabs, bsz, seq, w_in, s5_a_re, s5_a_im, s5_log_dt, s5_b_re,
           s5_b_im, s5_c_re, s5_c_im, s5_d, s5_w_glu, ml_conv_w, ml_wq, ml_wk, ml_b_ig, ml_b_fg,
           ml_gn, xa_w_kv, w_out, ln_g, ln_b):
    p_ml, p_s5, p_xa, p_ret, p_if = _project(x2, _reorder_w_in(w_in))
    ret = _retention(p_ret, cos, sin, ret_tabs, bsz, seq)
    s5w = _s5_weights(s5_a_re, s5_a_im, s5_log_dt, s5_b_re, s5_b_im, s5_c_re, s5_c_im, s5_d,
                      s5_w_glu, _s5_block_tokens(seq))
    s5o = _s5(p_s5, s5w, bsz, seq)
    bias = jnp.concatenate([ml_b_ig, ml_b_fg, jnp.zeros((W_IF - 2 * HEADS,), F32)])[None, :]
    ml = _mlstm(p_ml, p_if, (ml_conv_w, ml_wq.astype(BF16), ml_wk.astype(BF16), bias,
                             ml_gn[None, :]), bsz, seq)
    mk, mv = _memory_kv(mem2, xa_w_kv.astype(BF16))
    xa = _cross_attention(p_xa, mk, mv, bsz, seq)
    return _out_project(x2, (ret, s5o, ml, xa), w_out.astype(BF16), ln_g[None, :], ln_b[None, :])


def kernel(x, mem, positions, w_in, s5_a_re, s5_a_im, s5_log_dt, s5_b_re, s5_b_im, s5_c_re, s5_c_im,
           s5_d, s5_w_glu, ml_conv_w, ml_wq, ml_wk, ml_b_ig, ml_b_fg, ml_gn, xa_w_kv, w_out, ln_g,
           ln_b):
    bsz, seq, _ = x.shape
    x2 = x.reshape(bsz * seq, D_MODEL)
    mem2 = mem.reshape(bsz * N_MEM, D_MODEL)
    cos, sin = _rope_tables(positions)
    ret_tabs = _retention_tables()
    layer_params = (w_in, s5_a_re, s5_a_im, s5_log_dt, s5_b_re, s5_b_im, s5_c_re, s5_c_im, s5_d,
                    s5_w_glu, ml_conv_w, ml_wq, ml_wk, ml_b_ig, ml_b_fg, ml_gn, xa_w_kv, w_out,
                    ln_g, ln_b)
    for l in range(w_in.shape[0]):
        x2 = _layer(x2, mem2, cos, sin, ret_tabs, bsz, seq, *(p[l] for p in layer_params))
    return x2.reshape(bsz, seq, D_MODEL)
```

```python
import functools
import math

import jax
import jax.numpy as jnp
from jax import lax
from jax.experimental import pallas as pl
from jax.experimental.pallas import tpu as pltpu

F32 = jnp.float32
BF16 = jnp.bfloat16

D_MODEL = 1024
CHUNK = 64
N_MEM = 256
GROUP_W = 512
HEADS = 4
RET_DK = 64
RET_DV = 128
ROPE_BASE = 10000.0
S5_CH = 16
S5_GROUPS = 32
S5_STATE = 64
S5_LANES = S5_GROUPS * S5_STATE
ML_DH = 128
ML_CONV = 4
XA_DH = 128
LN_EPS = 1e-5
GN_EPS = 1e-6
DEPTH = 4
ALPHA = (2 * DEPTH) ** 0.25

SUBLANES = 8
VMEM_LIMIT = 56 * 1024 * 1024

W_ML, W_S5, W_XA, W_RET, W_IF = 4 * GROUP_W, 2 * GROUP_W, 2 * GROUP_W, 3 * GROUP_W, 128


def _params(sem):
    return pltpu.CompilerParams(dimension_semantics=sem, vmem_limit_bytes=VMEM_LIMIT)


def _sigmoid(x):
    return 1.0 / (1.0 + jnp.exp(-x))


def _silu(x):
    return x * _sigmoid(x)


def _norm_rows(t, eps):
    mu = jnp.mean(t, axis=-1, keepdims=True)
    d = t - mu
    var = jnp.mean(d * d, axis=-1, keepdims=True)
    return d * lax.rsqrt(var + eps)


def _dot(a, b):
    return jnp.dot(a, b, preferred_element_type=F32)


def _dot_nt(a, b):
    return lax.dot_general(a, b, (((1,), (1,)), ((), ())), preferred_element_type=F32)


def _dot_tn(a, b):
    return lax.dot_general(a, b, (((0,), (0,)), ((), ())), preferred_element_type=F32)


def _rope_kernel(pos_ref, inv_ref, cos_ref, sin_ref):
    ang = pos_ref[...].astype(F32) * inv_ref[...]
    cos_ref[...] = jnp.cos(ang)
    sin_ref[...] = jnp.sin(ang)


def _rope_tables(positions):
    m = positions.size
    half = RET_DK // 2
    inv = ROPE_BASE ** (-jnp.arange(half, dtype=F32) / half)
    inv = jnp.tile(inv, HEADS)[None, :]
    tm = min(512, m)
    return pl.pallas_call(
        _rope_kernel,
        grid=(m // tm,),
        in_specs=[pl.BlockSpec((tm, 1), lambda i: (i, 0)),
                  pl.BlockSpec((1, 128), lambda i: (0, 0))],
        out_specs=[pl.BlockSpec((tm, 128), lambda i: (i, 0))] * 2,
        out_shape=[jax.ShapeDtypeStruct((m, 128), F32)] * 2,
        compiler_params=_params(("parallel",)),
        name="rope_tables",
    )(positions.reshape(m, 1), inv)


def _proj_kernel(x_ref, w_ref, *o_refs):
    xb = x_ref[...].astype(BF16)
    off = 0
    for o in o_refs:
        if len(o.shape) == 3:
            for j in range(o.shape[0]):
                o[j] = _dot(xb, w_ref[:, off:off + 128])
                off += 128
        else:
            n = o.shape[1]
            o[...] = _dot(xb, w_ref[:, off:off + n])
            off += n


def _project(x2, w_in_r):
    m = x2.shape[0]
    tm = min(256, m)
    row = lambda i: (i, 0)
    specs = [pl.BlockSpec((tm, W_ML), row),
             pl.BlockSpec((W_S5 // 128, tm, 128), lambda i: (0, i, 0)),
             pl.BlockSpec((tm, W_XA), row),
             pl.BlockSpec((tm, W_RET), row),
             pl.BlockSpec((tm, W_IF), row)]
    shapes = [(m, W_ML), (W_S5 // 128, m, 128), (m, W_XA), (m, W_RET), (m, W_IF)]
    return pl.pallas_call(
        _proj_kernel,
        grid=(m // tm,),
        in_specs=[pl.BlockSpec((tm, D_MODEL), row),
                  pl.BlockSpec(w_in_r.shape, lambda i: (0, 0))],
        out_specs=specs,
        out_shape=[jax.ShapeDtypeStruct(s, F32) for s in shapes],
        compiler_params=_params(("parallel",)),
        name="in_proj",
    )(x2, w_in_r)


def _kv_kernel(m_ref, w_ref, k_ref, v_ref):
    mb = m_ref[...].astype(BF16)
    k_ref[...] = _dot(mb, w_ref[:, :GROUP_W]).astype(BF16)
    v_ref[...] = _dot(mb, w_ref[:, GROUP_W:]).astype(BF16)


def _memory_kv(mem2, w_kv):
    m = mem2.shape[0]
    return pl.pallas_call(
        _kv_kernel,
        grid=(m // N_MEM,),
        in_specs=[pl.BlockSpec((N_MEM, D_MODEL), lambda i: (i, 0)),
                  pl.BlockSpec(w_kv.shape, lambda i: (0, 0))],
        out_specs=[pl.BlockSpec((N_MEM, GROUP_W), lambda i: (i, 0))] * 2,
        out_shape=[jax.ShapeDtypeStruct((m, GROUP_W), BF16)] * 2,
        compiler_params=_params(("parallel",)),
        name="memory_kv",
    )(mem2, w_kv)


def _ret_kernel(p_ref, cos_ref, sin_ref, hmask_ref, intra_ref, qdec_ref, kdec_ref, cdec_ref,
                o_ref, s_ref):
    @pl.when(pl.program_id(1) == 0)
    def _():
        s_ref[...] = jnp.zeros_like(s_ref)

    cs = cos_ref[...]
    sn = sin_ref[...]

    def rot(t):
        t1 = t[:, :128]
        t2 = t[:, 128:]
        return jnp.concatenate([t1 * cs - t2 * sn, t1 * sn + t2 * cs], axis=1)

    qr = rot(p_ref[:, 0:256])
    kr = rot(p_ref[:, 256:512])
    vb = p_ref[:, 512:1024].astype(BF16)
    z = p_ref[:, 1024:1536]
    krb = kr.astype(BF16)
    state = s_ref[...]
    sb = state.astype(BF16)
    outs = []
    for h in range(HEADS):
        cols = slice(RET_DV * h, RET_DV * (h + 1))
        qm = (qr * hmask_ref[h:h + 1, :]).astype(BF16)
        att = (_dot_nt(qm, krb) * intra_ref[h]).astype(BF16)
        o = _dot(att, vb[:, cols]) + qdec_ref[:, cols] * _dot(qm, sb[:, cols])
        outs.append(_norm_rows(o, GN_EPS))
    o_ref[...] = jnp.concatenate(outs, axis=1) * _silu(z)
    kd = (kr * kdec_ref[...]).astype(BF16)
    s_ref[...] = cdec_ref[...] * state + _dot_tn(kd, vb)


def _retention(p_ret, cos, sin, tabs, bsz, seq):
    nc = seq // CHUNK
    row = lambda b, c: (b * nc + c, 0)
    const2 = lambda b, c: (0, 0)
    hmask, intra, qdec, kdec, cdec = tabs
    return pl.pallas_call(
        _ret_kernel,
        grid=(bsz, nc),
        in_specs=[pl.BlockSpec((CHUNK, W_RET), row),
                  pl.BlockSpec((CHUNK, 128), row),
                  pl.BlockSpec((CHUNK, 128), row),
                  pl.BlockSpec(hmask.shape, const2),
                  pl.BlockSpec(intra.shape, lambda b, c: (0, 0, 0)),
                  pl.BlockSpec(qdec.shape, const2),
                  pl.BlockSpec(kdec.shape, const2),
                  pl.BlockSpec(cdec.shape, const2)],
        out_specs=pl.BlockSpec((CHUNK, GROUP_W), row),
        out_shape=jax.ShapeDtypeStruct((bsz * seq, GROUP_W), F32),
        scratch_shapes=[pltpu.VMEM((HEADS * RET_DK, HEADS * RET_DV), F32)],
        compiler_params=_params(("parallel", "arbitrary")),
        name="retention",
    )(p_ret, cos, sin, hmask, intra, qdec, kdec, cdec)


def _retention_tables():
    log_g = jnp.log1p(-jnp.exp2(-5.0 - jnp.arange(HEADS, dtype=F32)))
    r = jnp.arange(CHUNK, dtype=F32)
    intra = jnp.exp(jnp.abs(r[:, None] - r[None, :]) * log_g[:, None, None])
    q_dec = jnp.exp((r + 1.0) * log_g[:, None])
    k_dec = jnp.exp((CHUNK - 1.0 - r) * log_g[:, None])
    c_dec = jnp.exp(CHUNK * log_g)
    lane_head = (jnp.arange(HEADS * RET_DK) % 128) // (RET_DK // 2)
    hmask = (lane_head[None, :] == jnp.arange(HEADS)[:, None]).astype(F32)
    qdec = jnp.repeat(q_dec.T, RET_DV, axis=1)
    kdec = k_dec.T[:, lane_head]
    cdec = jnp.repeat(c_dec, RET_DV)[None, :]
    return hmask, intra, qdec, kdec, cdec


def _gelu_tanh(x):
    return 0.5 * x * (1.0 + jnp.tanh(math.sqrt(2.0 / math.pi) * (x + 0.044715 * (x * x * x))))


S5_LANE_BLOCK = 512


def _s5_kernel(p_ref, bbd_ref, cbd_ref, a_ref, as_ref, d_ref, wglu_ref, o_ref, bu_ref, carry_ref,
               *, sub):
    @pl.when(pl.program_id(1) == 0)
    def _():
        carry_ref[...] = jnp.zeros_like(carry_ref)

    def gather(j):
        return jnp.concatenate([p_ref[j, pl.ds(s, SUBLANES, stride=sub), :] for s in range(sub)],
                               axis=0)

    nblk = GROUP_W // 128
    u = jnp.concatenate([gather(j) for j in range(nblk)], axis=1)
    z = jnp.concatenate([gather(nblk + j) for j in range(nblk)], axis=1)
    bu_ref[...] = _dot(u.astype(BF16), bbd_ref[...])

    ns = S5_LANES
    lb = S5_LANE_BLOCK
    for blk in range(ns // lb):
        re = slice(blk * lb, (blk + 1) * lb)
        im = slice(ns + blk * lb, ns + (blk + 1) * lb)
        ar = jnp.broadcast_to(a_ref[0:1, re], (SUBLANES, lb))
        ai = jnp.broadcast_to(a_ref[1:2, re], (SUBLANES, lb))

        def step(s, xr, xi):
            rows = pl.ds(pl.multiple_of(s * SUBLANES, SUBLANES), SUBLANES)
            nr = ar * xr - ai * xi + bu_ref[rows, re]
            ni = ar * xi + ai * xr + bu_ref[rows, im]
            return rows, nr, ni

        def end_state(s, carry):
            _, nr, ni = step(s, *carry)
            return nr, ni

        zero = jnp.zeros((SUBLANES, lb), F32)
        er, ei = lax.fori_loop(0, sub, end_state, (zero, zero), unroll=8)

        asr = as_ref[0:1, re]
        asi = as_ref[1:2, re]
        xr = carry_ref[0:1, re]
        xi = carry_ref[1:2, re]
        start_r, start_i = [], []
        for k in range(SUBLANES):
            start_r.append(xr)
            start_i.append(xi)
            xr, xi = (asr * xr - asi * xi + er[k:k + 1, :], asr * xi + asi * xr + ei[k:k + 1, :])
        carry_ref[0:1, re] = xr
        carry_ref[1:2, re] = xi

        def write_states(s, carry):
            rows, nr, ni = step(s, *carry)
            bu_ref[rows, re] = nr
            bu_ref[rows, im] = ni
            return nr, ni

        lax.fori_loop(0, sub, write_states,
                      (jnp.concatenate(start_r, axis=0), jnp.concatenate(start_i, axis=0)), unroll=8)

    y = _dot(bu_ref[...].astype(BF16), cbd_ref[...]) + d_ref[...] * u
    y = _gelu_tanh(y)
    y = y * _sigmoid(_dot(y.astype(BF16), wglu_ref[...]))
    res = y * _silu(z)
    for j in range(nblk):
        for s in range(sub):
            o_ref[j, pl.ds(s, SUBLANES, stride=sub), :] = (
                res[s * SUBLANES:(s + 1) * SUBLANES, 128 * j:128 * (j + 1)])


def _s5(p_s5, w, bsz, seq):
    bbd, cbd, a, a_sub, d, wglu, t_blk = w
    nb = seq // t_blk
    row = lambda b, j: (0, b * nb + j, 0)
    const2 = lambda b, j: (0, 0)
    return pl.pallas_call(
        functools.partial(_s5_kernel, sub=t_blk // SUBLANES),
        grid=(bsz, nb),
        in_specs=[pl.BlockSpec((W_S5 // 128, t_blk, 128), row)] +
                 [pl.BlockSpec(t.shape, const2) for t in (bbd, cbd, a, a_sub, d, wglu)],
        out_specs=pl.BlockSpec((GROUP_W // 128, t_blk, 128), row),
        out_shape=jax.ShapeDtypeStruct((GROUP_W // 128, bsz * seq, 128), F32),
        scratch_shapes=[pltpu.VMEM((t_blk, 2 * S5_LANES), F32),
                        pltpu.VMEM((2, S5_LANES), F32)],
        compiler_params=_params(("parallel", "arbitrary")),
        name="s5",
    )(p_s5, bbd, cbd, a, a_sub, d, wglu)


def _s5_block_tokens(seq):
    t = min(512, seq)
    assert seq % t == 0 and t % (SUBLANES * SUBLANES) == 0
    return t


def _s5_weights(a_re, a_im, log_dt, b_re, b_im, c_re, c_im, d, w_glu, t_blk):
    dt = jnp.exp(log_dt)[:, None]
    mag = jnp.exp(a_re * dt)
    ab_re = mag * jnp.cos(a_im * dt)
    ab_im = mag * jnp.sin(a_im * dt)
    den = a_re * a_re + a_im * a_im
    p = ab_re - 1.0
    f_re = (p * a_re + ab_im * a_im) / den
    f_im = (ab_im * a_re - p * a_im) / den
    bb_re = f_re[..., None] * b_re - f_im[..., None] * b_im
    bb_im = f_re[..., None] * b_im + f_im[..., None] * b_re
    eye = jnp.eye(S5_GROUPS, dtype=F32)

    def blockdiag_in(t):
        return (eye[:, None, :, None] * jnp.swapaxes(t, 1, 2)[:, :, None, :]).reshape(GROUP_W, S5_LANES)

    def blockdiag_out(t):
        return (eye[:, None, :, None] * jnp.swapaxes(t, 1, 2)[:, :, None, :]).reshape(S5_LANES, GROUP_W)

    bbd = jnp.concatenate([blockdiag_in(bb_re), blockdiag_in(bb_im)], axis=1).astype(BF16)
    cbd = jnp.concatenate([blockdiag_out(c_re), blockdiag_out(-c_im)], axis=0).astype(BF16)
    sr, si = ab_re, ab_im
    for _ in range(int(math.log2(t_blk // SUBLANES))):
        sr, si = sr * sr - si * si, 2.0 * sr * si
    a = jnp.stack([ab_re.reshape(-1), ab_im.reshape(-1)])
    a_sub = jnp.stack([sr.reshape(-1), si.reshape(-1)])
    return bbd, cbd, a, a_sub, d[None, :], w_glu.astype(BF16), t_blk


def _log_sigmoid(x):
    return jnp.minimum(x, 0.0) - jnp.log1p(jnp.exp(-jnp.abs(x)))


ML_BLOCK_CHUNKS = 4


def _segmented_scan(x, row_in_chunk, combine, identity):
    sh = 1
    while sh < CHUNK:
        x = combine(x, jnp.where(row_in_chunk >= sh, pltpu.roll(x, sh, 0), identity))
        sh *= 2
    return x


def _time_on_lanes(t):
    return jnp.concatenate([t, jnp.zeros_like(t)], axis=0).T


def _ml_kernel(p_ref, g_ref, conv_ref, wq_ref, wkt_ref, bias_ref, gn_ref, o_ref,
               xbuf_ref, q_ref, kt_ref, rep_ref, cn_ref, m_ref):
    t_blk = p_ref.shape[0]
    n_chunks = t_blk // CHUNK

    @pl.when(pl.program_id(1) == 0)
    def _():
        xbuf_ref[0:SUBLANES, :] = jnp.zeros((SUBLANES, GROUP_W), F32)
        cn_ref[...] = jnp.zeros_like(cn_ref)
        m_ref[...] = jnp.zeros_like(m_ref)

    xbuf_ref[SUBLANES:, :] = p_ref[:, 0:GROUP_W]
    conv = jnp.zeros((t_blk, GROUP_W), F32)
    for j in range(ML_CONV):
        off = SUBLANES - (ML_CONV - 1) + j
        conv = conv + conv_ref[j:j + 1, :] * xbuf_ref[off:off + t_blk, :]
    xbuf_ref[0:SUBLANES, :] = xbuf_ref[t_blk:t_blk + SUBLANES, :]
    xc = _silu(conv).astype(BF16)
    for h in range(HEADS):
        cols = slice(ML_DH * h, ML_DH * (h + 1))
        q_ref[:, cols] = _dot(xc[:, cols], wq_ref[h]).astype(BF16)
        kt_ref[h] = _dot_nt(wkt_ref[h], xc[:, cols]) * ML_DH ** -0.5

    gates = g_ref[...] + bias_ref[...]
    ig = gates
    lf = _log_sigmoid(pltpu.roll(gates, 128 - HEADS, 1))
    row_in_chunk = lax.broadcasted_iota(jnp.int32, (t_blk, 128), 0) % CHUNK
    bcum = _segmented_scan(lf, row_in_chunk, jnp.add, 0.0)
    dgate = ig - bcum
    cmax = _segmented_scan(dgate, row_in_chunk, jnp.maximum, -jnp.inf)
    m_loc = bcum + cmax
    m_prev = m_ref[0:1, :]
    w_intra, w_inter, floor, wk_t, dec_scale = [], [], [], [], []
    for c in range(n_chunks):
        rows = slice(CHUNK * c, CHUNK * (c + 1))
        last = slice(CHUNK * (c + 1) - 1, CHUNK * (c + 1))
        b_last = bcum[last]
        g_max = b_last + cmax[last]
        m_new = jnp.maximum(b_last + m_prev, g_max)
        dec_scale.append(jnp.exp(b_last + m_prev - m_new))
        dec_scale.append(jnp.exp(g_max - m_new))
        inter = bcum[rows] + m_prev
        m_r = jnp.maximum(m_loc[rows], inter)
        w_inter.append(jnp.exp(inter - m_r))
        w_intra.append(jnp.exp(m_loc[rows] - m_r))
        floor.append(jnp.exp(-m_r))
        wk_t.append(jnp.exp(b_last - bcum[rows] + ig[rows] - g_max))
        m_prev = m_new
    m_ref[0:1, :] = m_prev
    pad_rows = -2 * n_chunks % SUBLANES
    if pad_rows:
        dec_scale.append(jnp.zeros((pad_rows, 128), F32))
    per_time = [cmax, jnp.concatenate(w_intra, axis=0), jnp.concatenate(w_inter, axis=0),
                jnp.concatenate(floor, axis=0), jnp.concatenate(dec_scale, axis=0)]
    offs = [0]
    for t in per_time:
        offs.append(offs[-1] + t.shape[0])
    for h in range(HEADS):
        for t, o in zip(per_time, offs):
            rep_ref[h, o:o + t.shape[0], :] = jnp.broadcast_to(t[:, h:h + 1], t.shape)

    r_i = lax.broadcasted_iota(jnp.int32, (CHUNK, CHUNK), 0)
    c_i = lax.broadcasted_iota(jnp.int32, (CHUNK, CHUNK), 1)
    causal = c_i <= r_i
    ones = jnp.ones((CHUNK, ML_DH), BF16)
    for c in range(n_chunks):
        rows = slice(CHUNK * c, CHUNK * (c + 1))
        d_t = _time_on_lanes(dgate[rows])
        wk_tt = _time_on_lanes(wk_t[c])
        for h in range(HEADS):
            cols = slice(ML_DH * h, ML_DH * (h + 1))
            rep = lambda i: rep_ref[h, offs[i] + CHUNK * c:offs[i] + CHUNK * (c + 1), :]
            cmax_b, a_b, i_b, floor_b = rep(0), rep(1), rep(2), rep(3)
            dec_b = rep_ref[h, offs[4] + 2 * c:offs[4] + 2 * c + 1, :]
            scale_b = rep_ref[h, offs[4] + 2 * c + 1:offs[4] + 2 * c + 2, :]
            qb = q_ref[rows, cols]
            kt = kt_ref[h, :, rows]
            v_ext = jnp.concatenate(
                [p_ref[rows, GROUP_W + ML_DH * h:GROUP_W + ML_DH * (h + 1)].astype(BF16), ones], axis=1)
            cn_prev = cn_ref[h]

            w = jnp.exp(jnp.where(causal, d_t[h:h + 1, 0:CHUNK] - cmax_b[:, 0:CHUNK], -jnp.inf))
            s = _dot(qb, kt.astype(BF16)) * w
            intra = _dot(s.astype(BF16), v_ext)
            inter = _dot(qb, cn_prev.astype(BF16))
            both = jnp.concatenate([a_b, a_b], axis=1) * intra + jnp.concatenate([i_b, i_b], axis=1) * inter
            hh = both[:, 0:ML_DH] / jnp.maximum(jnp.abs(both[:, ML_DH:]), floor_b)

            ktw = (kt * wk_tt[h:h + 1, 0:CHUNK]).astype(BF16)
            cn_ref[h] = (jnp.concatenate([dec_b, dec_b], axis=1) * cn_prev
                         + jnp.concatenate([scale_b, scale_b], axis=1) * _dot(ktw, v_ext))

            og = p_ref[rows, 2 * GROUP_W + ML_DH * h:2 * GROUP_W + ML_DH * (h + 1)]
            z = p_ref[rows, 3 * GROUP_W + ML_DH * h:3 * GROUP_W + ML_DH * (h + 1)]
            o_ref[rows, cols] = (_norm_rows(hh * _sigmoid(og), GN_EPS) * gn_ref[:, cols] * _silu(z))


def _mlstm(p_ml, p_if, w, bsz, seq):
    conv_w, wq, wkt, bias, gn = w
    t_blk = min(ML_BLOCK_CHUNKS * CHUNK, seq)
    nb = seq // t_blk
    n_chunks = t_blk // CHUNK
    rep_rows = 4 * t_blk + 2 * n_chunks + (-2 * n_chunks % SUBLANES)
    row = lambda b, c: (b * nb + c, 0)
    const2 = lambda b, c: (0, 0)
    const3 = lambda b, c: (0, 0, 0)
    return pl.pallas_call(
        _ml_kernel,
        grid=(bsz, nb),
        in_specs=[pl.BlockSpec((t_blk, W_ML), row),
                  pl.BlockSpec((t_blk, W_IF), row),
                  pl.BlockSpec(conv_w.shape, const2),
                  pl.BlockSpec(wq.shape, const3),
                  pl.BlockSpec(wkt.shape, const3),
                  pl.BlockSpec(bias.shape, const2),
                  pl.BlockSpec(gn.shape, const2)],
        out_specs=pl.BlockSpec((t_blk, GROUP_W), row),
        out_shape=jax.ShapeDtypeStruct((bsz * seq, GROUP_W), F32),
        scratch_shapes=[pltpu.VMEM((t_blk + SUBLANES, GROUP_W), F32),
                        pltpu.VMEM((t_blk, GROUP_W), BF16),
                        pltpu.VMEM((HEADS, ML_DH, t_blk), F32),
                        pltpu.VMEM((HEADS, rep_rows, 128), F32),
                        pltpu.VMEM((HEADS, ML_DH, 2 * ML_DH), F32),
                        pltpu.VMEM((SUBLANES, 128), F32)],
        compiler_params=_params(("parallel", "arbitrary")),
        name="mlstm",
    )(p_ml, p_if, conv_w, wq, wkt, bias, gn)


def _xa_kernel(p_ref, mk_ref, mv_ref, o_ref):
    z = p_ref[:, GROUP_W:]
    outs = []
    for h in range(HEADS):
        cols = slice(XA_DH * h, XA_DH * (h + 1))
        s = _dot_nt(p_ref[:, cols].astype(BF16), mk_ref[:, cols]) * XA_DH ** -0.5
        e = jnp.exp(s - jnp.max(s, axis=1, keepdims=True))
        p = e / jnp.sum(e, axis=1, keepdims=True)
        outs.append(_dot(p.astype(BF16), mv_ref[:, cols]))
    o_ref[...] = jnp.concatenate(outs, axis=1) * _silu(z)


def _cross_attention(p_xa, mk, mv, bsz, seq):
    t_blk = min(256, seq)
    nb = seq // t_blk
    row = lambda b, j: (b * nb + j, 0)
    mem = lambda b, j: (b, 0)
    return pl.pallas_call(
        _xa_kernel,
        grid=(bsz, nb),
        in_specs=[pl.BlockSpec((t_blk, W_XA), row),
                  pl.BlockSpec((N_MEM, GROUP_W), mem),
                  pl.BlockSpec((N_MEM, GROUP_W), mem)],
        out_specs=pl.BlockSpec((t_blk, GROUP_W), row),
        out_shape=jax.ShapeDtypeStruct((bsz * seq, GROUP_W), F32),
        compiler_params=_params(("parallel", "parallel")),
        name="memory_attention",
    )(p_xa, mk, mv)


def _out_kernel(x_ref, r_ref, s_ref, m_ref, a_ref, w_ref, g_ref, b_ref, o_ref):
    y = None
    s5o = jnp.concatenate([s_ref[j] for j in range(s_ref.shape[0])], axis=1)
    for i, t in enumerate((r_ref[...], s5o, m_ref[...], a_ref[...])):
        part = _dot(t.astype(BF16), w_ref[GROUP_W * i:GROUP_W * (i + 1), :])
        y = part if y is None else y + part
    t = ALPHA * x_ref[...] + y
    o_ref[...] = _norm_rows(t, LN_EPS) * g_ref[...] + b_ref[...]


def _out_project(x2, groups, w_out, ln_g, ln_b):
    m = x2.shape[0]
    tm = min(256, m)
    row = lambda i: (i, 0)
    const = lambda i: (0, 0)
    return pl.pallas_call(
        _out_kernel,
        grid=(m // tm,),
        in_specs=[pl.BlockSpec((tm, D_MODEL), row),
                  pl.BlockSpec((tm, GROUP_W), row),
                  pl.BlockSpec((GROUP_W // 128, tm, 128), lambda i: (0, i, 0)),
                  pl.BlockSpec((tm, GROUP_W), row),
                  pl.BlockSpec((tm, GROUP_W), row),
                  pl.BlockSpec(w_out.shape, const),
                  pl.BlockSpec((1, D_MODEL), const),
                  pl.BlockSpec((1, D_MODEL), const)],
        out_specs=pl.BlockSpec((tm, D_MODEL), row),
        out_shape=jax.ShapeDtypeStruct((m, D_MODEL), F32),
        compiler_params=_params(("parallel",)),
        name="out_proj_norm",
    )(x2, *groups, w_out, ln_g, ln_b)


def _reorder_w_in(w_in):
    o = 0
    cols = {}
    for name, w in (("rq", 256), ("rk", 256), ("rv", 512), ("rz", 512), ("su", 512), ("sz", 512),
                    ("mx", 512), ("mv", 512), ("mo", 512), ("mz", 512), ("mi", 4), ("mf", 4),
                    ("aq", 512), ("az", 512)):
        cols[name] = w_in[:, o:o + w]
        o += w
    half = RET_DK // 2
    perm = jnp.array([h * RET_DK + s * half + j for s in range(2) for h in range(HEADS)
                      for j in range(half)], jnp.int32)
    rq = cols["rq"][:, perm]
    rk = cols["rk"][:, perm] * RET_DK ** -0.5
    pad = jnp.zeros((w_in.shape[0], W_IF - 2 * HEADS), w_in.dtype)
    return jnp.concatenate(
        [cols["mx"], cols["mv"], cols["mo"], cols["mz"], cols["su"], cols["sz"],
         cols["aq"], cols["az"], rq, rk, cols["rv"], cols["rz"], cols["mi"], cols["mf"], pad],
        axis=1).astype(BF16)


def _layer(x2, mem2, cos, sin, ret_tabs, bsz, seq, w_in, s5_a_re, s5_a_im, s5_log_dt, s5_b_re,
           s5_b_im, s5_c_re, s5_c_im, s5_d, s5_w_glu, ml_conv_w, ml_wq, ml_wk, ml_b_ig, ml_b_fg,
           ml_gn, xa_w_kv, w_out, ln_g, ln_b):
    p_ml, p_s5, p_xa, p_ret, p_if = _project(x2, _reorder_w_in(w_in))
    ret = _retention(p_ret, cos, sin, ret_tabs, bsz, seq)
    s5w = _s5_weights(s5_a_re, s5_a_im, s5_log_dt, s5_b_re, s5_b_im, s5_c_re, s5_c_im, s5_d,
                      s5_w_glu, _s5_block_tokens(seq))
    s5o = _s5(p_s5, s5w, bsz, seq)
    bias = jnp.concatenate([ml_b_ig, ml_b_fg, jnp.zeros((W_IF - 2 * HEADS,), F32)])[None, :]
    ml = _mlstm(p_ml, p_if, (ml_conv_w, ml_wq.astype(BF16),
                             jnp.swapaxes(ml_wk, 1, 2).astype(BF16), bias,
                             ml_gn[None, :]), bsz, seq)
    mk, mv = _memory_kv(mem2, xa_w_kv.astype(BF16))
    xa = _cross_attention(p_xa, mk, mv, bsz, seq)
    return _out_project(x2, (ret, s5o, ml, xa), w_out.astype(BF16), ln_g[None, :], ln_b[None, :])


def kernel(x, mem, positions, w_in, s5_a_re, s5_a_im, s5_log_dt, s5_b_re, s5_b_im, s5_c_re, s5_c_im,
           s5_d, s5_w_glu, ml_conv_w, ml_wq, ml_wk, ml_b_ig, ml_b_fg, ml_gn, xa_w_kv, w_out, ln_g,
           ln_b):
    bsz, seq, _ = x.shape
    x2 = x.reshape(bsz * seq, D_MODEL)
    mem2 = mem.reshape(bsz * N_MEM, D_MODEL)
    cos, sin = _rope_tables(positions)
    ret_tabs = _retention_tables()
    layer_params = (w_in, s5_a_re, s5_a_im, s5_log_dt, s5_b_re, s5_b_im, s5_c_re, s5_c_im, s5_d,
                    s5_w_glu, ml_conv_w, ml_wq, ml_wk, ml_b_ig, ml_b_fg, ml_gn, xa_w_kv, w_out,
                    ln_g, ln_b)
    for l in range(w_in.shape[0]):
        x2 = _layer(x2, mem2, cos, sin, ret_tabs, bsz, seq, *(p[l] for p in layer_params))
    return x2.reshape(bsz, seq, D_MODEL)
```

```python
import functools
import math

import jax
import jax.numpy as jnp
from jax import lax
from jax.experimental import pallas as pl
from jax.experimental.pallas import tpu as pltpu

F32 = jnp.float32
BF16 = jnp.bfloat16

D_MODEL = 1024
CHUNK = 64
N_MEM = 256
GROUP_W = 512
HEADS = 4
RET_DK = 64
RET_DV = 128
ROPE_BASE = 10000.0
S5_CH = 16
S5_GROUPS = 32
S5_STATE = 64
S5_LANES = S5_GROUPS * S5_STATE
ML_DH = 128
ML_CONV = 4
XA_DH = 128
LN_EPS = 1e-5
GN_EPS = 1e-6
DEPTH = 4
ALPHA = (2 * DEPTH) ** 0.25

SUBLANES = 8
VMEM_LIMIT = 56 * 1024 * 1024

W_ML, W_S5, W_XA, W_RET, W_IF = 4 * GROUP_W, 2 * GROUP_W, 2 * GROUP_W, 3 * GROUP_W, 128


def _params(sem):
    return pltpu.CompilerParams(dimension_semantics=sem, vmem_limit_bytes=VMEM_LIMIT)


def _sigmoid(x):
    return 1.0 / (1.0 + jnp.exp(-x))


def _silu(x):
    return x * _sigmoid(x)


def _norm_rows(t, eps):
    mu = jnp.mean(t, axis=-1, keepdims=True)
    d = t - mu
    var = jnp.mean(d * d, axis=-1, keepdims=True)
    return d * lax.rsqrt(var + eps)


def _dot(a, b):
    return jnp.dot(a, b, preferred_element_type=F32)


def _dot_nt(a, b):
    return lax.dot_general(a, b, (((1,), (1,)), ((), ())), preferred_element_type=F32)


def _dot_tn(a, b):
    return lax.dot_general(a, b, (((0,), (0,)), ((), ())), preferred_element_type=F32)


def _rope_kernel(pos_ref, inv_ref, cos_ref, sin_ref):
    ang = pos_ref[...].astype(F32) * inv_ref[...]
    cos_ref[...] = jnp.cos(ang)
    sin_ref[...] = jnp.sin(ang)


def _rope_tables(positions):
    m = positions.size
    half = RET_DK // 2
    inv = ROPE_BASE ** (-jnp.arange(half, dtype=F32) / half)
    inv = jnp.tile(inv, HEADS)[None, :]
    tm = min(512, m)
    return pl.pallas_call(
        _rope_kernel,
        grid=(m // tm,),
        in_specs=[pl.BlockSpec((tm, 1), lambda i: (i, 0)),
                  pl.BlockSpec((1, 128), lambda i: (0, 0))],
        out_specs=[pl.BlockSpec((tm, 128), lambda i: (i, 0))] * 2,
        out_shape=[jax.ShapeDtypeStruct((m, 128), F32)] * 2,
        compiler_params=_params(("parallel",)),
        name="rope_tables",
    )(positions.reshape(m, 1), inv)


def _proj_kernel(x_ref, w_ref, *o_refs):
    xb = x_ref[...].astype(BF16)
    off = 0
    for o in o_refs:
        if len(o.shape) == 3:
            for j in range(o.shape[0]):
                o[j] = _dot(xb, w_ref[:, off:off + 128])
                off += 128
        else:
            n = o.shape[1]
            o[...] = _dot(xb, w_ref[:, off:off + n])
            off += n


def _project(x2, w_in_r):
    m = x2.shape[0]
    tm = min(256, m)
    row = lambda i: (i, 0)
    specs = [pl.BlockSpec((tm, W_ML), row),
             pl.BlockSpec((W_S5 // 128, tm, 128), lambda i: (0, i, 0)),
             pl.BlockSpec((tm, W_XA), row),
             pl.BlockSpec((tm, W_RET), row),
             pl.BlockSpec((tm, W_IF), row)]
    shapes = [(m, W_ML), (W_S5 // 128, m, 128), (m, W_XA), (m, W_RET), (m, W_IF)]
    return pl.pallas_call(
        _proj_kernel,
        grid=(m // tm,),
        in_specs=[pl.BlockSpec((tm, D_MODEL), row),
                  pl.BlockSpec(w_in_r.shape, lambda i: (0, 0))],
        out_specs=specs,
        out_shape=[jax.ShapeDtypeStruct(s, F32) for s in shapes],
        compiler_params=_params(("parallel",)),
        name="in_proj",
    )(x2, w_in_r)


def _kv_kernel(m_ref, w_ref, k_ref, v_ref):
    mb = m_ref[...].astype(BF16)
    k_ref[...] = _dot(mb, w_ref[:, :GROUP_W]).astype(BF16)
    v_ref[...] = _dot(mb, w_ref[:, GROUP_W:]).astype(BF16)


def _memory_kv(mem2, w_kv):
    m = mem2.shape[0]
    return pl.pallas_call(
        _kv_kernel,
        grid=(m // N_MEM,),
        in_specs=[pl.BlockSpec((N_MEM, D_MODEL), lambda i: (i, 0)),
                  pl.BlockSpec(w_kv.shape, lambda i: (0, 0))],
        out_specs=[pl.BlockSpec((N_MEM, GROUP_W), lambda i: (i, 0))] * 2,
        out_shape=[jax.ShapeDtypeStruct((m, GROUP_W), BF16)] * 2,
        compiler_params=_params(("parallel",)),
        name="memory_kv",
    )(mem2, w_kv)


RET_BLOCK_CHUNKS = 4


def _ret_kernel(p_ref, cos_ref, sin_ref, hmask_ref, intra_ref, qdec_ref, kdec_ref, cdec_ref,
                o_ref, s_ref):
    @pl.when(pl.program_id(1) == 0)
    def _():
        s_ref[...] = jnp.zeros_like(s_ref)

    cs = cos_ref[...]
    sn = sin_ref[...]

    def rot(t):
        t1 = t[:, :128]
        t2 = t[:, 128:]
        return jnp.concatenate([t1 * cs - t2 * sn, t1 * sn + t2 * cs], axis=1)

    qr = rot(p_ref[:, 0:256])
    kr = rot(p_ref[:, 256:512])
    for c in range(p_ref.shape[0] // CHUNK):
        rows = slice(CHUNK * c, CHUNK * (c + 1))
        vb = p_ref[rows, 512:1024].astype(BF16)
        krb = kr[rows].astype(BF16)
        state = s_ref[...]
        sb = state.astype(BF16)
        for h in range(HEADS):
            cols = slice(RET_DV * h, RET_DV * (h + 1))
            qm = (qr[rows] * hmask_ref[h:h + 1, :]).astype(BF16)
            att = (_dot_nt(qm, krb) * intra_ref[h]).astype(BF16)
            o = _dot(att, vb[:, cols]) + qdec_ref[:, cols] * _dot(qm, sb[:, cols])
            z = p_ref[rows, 1024 + RET_DV * h:1024 + RET_DV * (h + 1)]
            o_ref[rows, cols] = _norm_rows(o, GN_EPS) * _silu(z)
        kd = (kr[rows] * kdec_ref[...]).astype(BF16)
        s_ref[...] = cdec_ref[...] * state + _dot_tn(kd, vb)


def _retention(p_ret, cos, sin, tabs, bsz, seq):
    t_blk = min(RET_BLOCK_CHUNKS * CHUNK, seq)
    nb = seq // t_blk
    row = lambda b, c: (b * nb + c, 0)
    const2 = lambda b, c: (0, 0)
    hmask, intra, qdec, kdec, cdec = tabs
    return pl.pallas_call(
        _ret_kernel,
        grid=(bsz, nb),
        in_specs=[pl.BlockSpec((t_blk, W_RET), row),
                  pl.BlockSpec((t_blk, 128), row),
                  pl.BlockSpec((t_blk, 128), row),
                  pl.BlockSpec(hmask.shape, const2),
                  pl.BlockSpec(intra.shape, lambda b, c: (0, 0, 0)),
                  pl.BlockSpec(qdec.shape, const2),
                  pl.BlockSpec(kdec.shape, const2),
                  pl.BlockSpec(cdec.shape, const2)],
        out_specs=pl.BlockSpec((t_blk, GROUP_W), row),
        out_shape=jax.ShapeDtypeStruct((bsz * seq, GROUP_W), F32),
        scratch_shapes=[pltpu.VMEM((HEADS * RET_DK, HEADS * RET_DV), F32)],
        compiler_params=_params(("parallel", "arbitrary")),
        name="retention",
    )(p_ret, cos, sin, hmask, intra, qdec, kdec, cdec)


def _retention_tables():
    log_g = jnp.log1p(-jnp.exp2(-5.0 - jnp.arange(HEADS, dtype=F32)))
    r = jnp.arange(CHUNK, dtype=F32)
    intra = jnp.exp(jnp.abs(r[:, None] - r[None, :]) * log_g[:, None, None])
    q_dec = jnp.exp((r + 1.0) * log_g[:, None])
    k_dec = jnp.exp((CHUNK - 1.0 - r) * log_g[:, None])
    c_dec = jnp.exp(CHUNK * log_g)
    lane_head = (jnp.arange(HEADS * RET_DK) % 128) // (RET_DK // 2)
    hmask = (lane_head[None, :] == jnp.arange(HEADS)[:, None]).astype(F32)
    qdec = jnp.repeat(q_dec.T, RET_DV, axis=1)
    kdec = k_dec.T[:, lane_head]
    cdec = jnp.repeat(c_dec, RET_DV)[None, :]
    return hmask, intra, qdec, kdec, cdec


def _gelu_tanh(x):
    return 0.5 * x * (1.0 + jnp.tanh(math.sqrt(2.0 / math.pi) * (x + 0.044715 * (x * x * x))))


S5_LANE_BLOCK = 512
S5_KTILE_GROUPS = 16
S5_KTILE_IN = S5_KTILE_GROUPS * S5_CH
S5_KTILE_LANES = S5_KTILE_GROUPS * S5_STATE
S5_KTILES = S5_GROUPS // S5_KTILE_GROUPS


def _s5_kernel(p_ref, bbd_ref, cbd_ref, a_ref, as_ref, d_ref, wglu_ref, o_ref, bu_ref, carry_ref,
               *, sub):
    @pl.when(pl.program_id(1) == 0)
    def _():
        carry_ref[...] = jnp.zeros_like(carry_ref)

    def gather(j):
        return jnp.concatenate([p_ref[j, pl.ds(s, SUBLANES, stride=sub), :] for s in range(sub)],
                               axis=0)

    nblk = GROUP_W // 128
    u = jnp.concatenate([gather(j) for j in range(nblk)], axis=1)
    z = jnp.concatenate([gather(nblk + j) for j in range(nblk)], axis=1)
    ub = u.astype(BF16)
    kw = 2 * S5_KTILE_LANES
    for kt in range(S5_KTILES):
        bu_ref[:, kt * kw:(kt + 1) * kw] = _dot(ub[:, kt * S5_KTILE_IN:(kt + 1) * S5_KTILE_IN],
                                                bbd_ref[kt])

    lb = S5_LANE_BLOCK
    for kt in range(S5_KTILES):
        for blk in range(S5_KTILE_LANES // lb):
            re = slice(kt * kw + blk * lb, kt * kw + (blk + 1) * lb)
            im = slice(kt * kw + S5_KTILE_LANES + blk * lb, kt * kw + S5_KTILE_LANES + (blk + 1) * lb)
            par = slice(kt * S5_KTILE_LANES + blk * lb, kt * S5_KTILE_LANES + (blk + 1) * lb)
            ar = jnp.broadcast_to(a_ref[0:1, par], (SUBLANES, lb))
            ai = jnp.broadcast_to(a_ref[1:2, par], (SUBLANES, lb))

            def step(s, xr, xi):
                rows = pl.ds(pl.multiple_of(s * SUBLANES, SUBLANES), SUBLANES)
                return (rows, ar * xr - ai * xi + bu_ref[rows, re], ar * xi + ai * xr + bu_ref[rows, im])

            def end_state(s, carry):
                return step(s, *carry)[1:]

            zero = jnp.zeros((SUBLANES, lb), F32)
            er, ei = lax.fori_loop(0, sub, end_state, (zero, zero), unroll=8)

            asr = as_ref[0:1, par]
            asi = as_ref[1:2, par]
            xr = carry_ref[0:1, par]
            xi = carry_ref[1:2, par]
            start_r, start_i = [], []
            for k in range(SUBLANES):
                start_r.append(xr)
                start_i.append(xi)
                xr, xi = (asr * xr - asi * xi + er[k:k + 1, :], asr * xi + asi * xr + ei[k:k + 1, :])
            carry_ref[0:1, par] = xr
            carry_ref[1:2, par] = xi

            def write_states(s, carry):
                rows, nr, ni = step(s, *carry)
                bu_ref[rows, re] = nr
                bu_ref[rows, im] = ni
                return nr, ni

            lax.fori_loop(0, sub, write_states,
                          (jnp.concatenate(start_r, axis=0), jnp.concatenate(start_i, axis=0)),
                          unroll=8)

    y = jnp.concatenate([_dot(bu_ref[:, kt * kw:(kt + 1) * kw].astype(BF16), cbd_ref[kt])
                         for kt in range(S5_KTILES)], axis=1) + d_ref[...] * u
    y = _gelu_tanh(y)
    y = y * _sigmoid(_dot(y.astype(BF16), wglu_ref[...]))
    res = y * _silu(z)
    for j in range(nblk):
        for s in range(sub):
            o_ref[j, pl.ds(s, SUBLANES, stride=sub), :] = (
                res[s * SUBLANES:(s + 1) * SUBLANES, 128 * j:128 * (j + 1)])


def _s5(p_s5, w, t_blk, bsz, seq):
    nb = seq // t_blk
    row = lambda b, j: (0, b * nb + j, 0)
    return pl.pallas_call(
        functools.partial(_s5_kernel, sub=t_blk // SUBLANES),
        grid=(bsz, nb),
        in_specs=[pl.BlockSpec((W_S5 // 128, t_blk, 128), row)] +
                 [pl.BlockSpec(t.shape, lambda b, j, nd=t.ndim: (0,) * nd) for t in w],
        out_specs=pl.BlockSpec((GROUP_W // 128, t_blk, 128), row),
        out_shape=jax.ShapeDtypeStruct((GROUP_W // 128, bsz * seq, 128), F32),
        scratch_shapes=[pltpu.VMEM((t_blk, 2 * S5_LANES), F32),
                        pltpu.VMEM((2, S5_LANES), F32)],
        compiler_params=_params(("parallel", "arbitrary")),
        name="s5",
    )(p_s5, *w)


def _s5_block_tokens(seq):
    t = min(512, seq)
    assert seq % t == 0 and t % (SUBLANES * SUBLANES) == 0
    return t


def _s5_weights(a_re, a_im, log_dt, b_re, b_im, c_re, c_im, d, w_glu, t_blk):
    dt = jnp.exp(log_dt)[:, None]
    mag = jnp.exp(a_re * dt)
    ab_re = mag * jnp.cos(a_im * dt)
    ab_im = mag * jnp.sin(a_im * dt)
    den = a_re * a_re + a_im * a_im
    p = ab_re - 1.0
    f_re = (p * a_re + ab_im * a_im) / den
    f_im = (ab_im * a_re - p * a_im) / den
    bb_re = f_re[..., None] * b_re - f_im[..., None] * b_im
    bb_im = f_re[..., None] * b_im + f_im[..., None] * b_re
    kg = S5_KTILE_GROUPS
    own_group = (jnp.arange(kg)[:, None, None, None] == jnp.arange(kg)[None, None, :, None])

    def tiles_in(t):
        t = jnp.swapaxes(t, 1, 2).reshape(S5_KTILES, kg, S5_CH, 1, S5_STATE)
        return jnp.where(own_group, t, 0.0).reshape(S5_KTILES, S5_KTILE_IN, S5_KTILE_LANES)

    def tiles_out(t):
        t = jnp.swapaxes(t, 1, 2).reshape(S5_KTILES, kg, S5_STATE, 1, S5_CH)
        return jnp.where(own_group, t, 0.0).reshape(S5_KTILES, S5_KTILE_LANES, S5_KTILE_IN)

    bbd = jnp.concatenate([tiles_in(bb_re), tiles_in(bb_im)], axis=2).astype(BF16)
    cbd = jnp.concatenate([tiles_out(c_re), tiles_out(-c_im)], axis=1).astype(BF16)
    sr, si = ab_re, ab_im
    for _ in range(int(math.log2(t_blk // SUBLANES))):
        sr, si = sr * sr - si * si, 2.0 * sr * si
    a = jnp.stack([ab_re.reshape(-1), ab_im.reshape(-1)])
    a_sub = jnp.stack([sr.reshape(-1), si.reshape(-1)])
    return bbd, cbd, a, a_sub, d[None, :], w_glu.astype(BF16)


def _log_sigmoid(x):
    return jnp.minimum(x, 0.0) - jnp.log1p(jnp.exp(-jnp.abs(x)))


ML_BLOCK_CHUNKS = 4


def _segmented_scan(x, row_in_chunk, combine, identity):
    sh = 1
    while sh < CHUNK:
        x = combine(x, jnp.where(row_in_chunk >= sh, pltpu.roll(x, sh, 0), identity))
        sh *= 2
    return x


def _time_on_lanes(t):
    return jnp.concatenate([t, jnp.zeros_like(t)], axis=0).T


def _ml_kernel(p_ref, g_ref, conv_ref, wq_ref, wkt_ref, bias_ref, gn_ref, o_ref,
               xbuf_ref, q_ref, kt_ref, rep_ref, cn_ref, m_ref):
    t_blk = p_ref.shape[0]
    n_chunks = t_blk // CHUNK

    @pl.when(pl.program_id(1) == 0)
    def _():
        xbuf_ref[0:SUBLANES, :] = jnp.zeros((SUBLANES, GROUP_W), F32)
        cn_ref[...] = jnp.zeros_like(cn_ref)
        m_ref[...] = jnp.zeros_like(m_ref)

    xbuf_ref[SUBLANES:, :] = p_ref[:, 0:GROUP_W]
    conv = jnp.zeros((t_blk, GROUP_W), F32)
    for j in range(ML_CONV):
        off = SUBLANES - (ML_CONV - 1) + j
        conv = conv + conv_ref[j:j + 1, :] * xbuf_ref[off:off + t_blk, :]
    xbuf_ref[0:SUBLANES, :] = xbuf_ref[t_blk:t_blk + SUBLANES, :]
    xc = _silu(conv).astype(BF16)
    for h in range(HEADS):
        cols = slice(ML_DH * h, ML_DH * (h + 1))
        q_ref[:, cols] = _dot(xc[:, cols], wq_ref[h]).astype(BF16)
        kt_ref[h] = _dot_nt(wkt_ref[h], xc[:, cols]) * ML_DH ** -0.5

    gates = g_ref[...] + bias_ref[...]
    ig = gates
    lf = _log_sigmoid(pltpu.roll(gates, 128 - HEADS, 1))
    row_in_chunk = lax.broadcasted_iota(jnp.int32, (t_blk, 128), 0) % CHUNK
    bcum = _segmented_scan(lf, row_in_chunk, jnp.add, 0.0)
    dgate = ig - bcum
    cmax = _segmented_scan(dgate, row_in_chunk, jnp.maximum, -jnp.inf)
    m_loc = bcum + cmax
    m_prev = m_ref[0:1, :]
    w_intra, w_inter, floor, wk_t, dec_scale = [], [], [], [], []
    for c in range(n_chunks):
        rows = slice(CHUNK * c, CHUNK * (c + 1))
        last = slice(CHUNK * (c + 1) - 1, CHUNK * (c + 1))
        b_last = bcum[last]
        g_max = b_last + cmax[last]
        m_new = jnp.maximum(b_last + m_prev, g_max)
        dec_scale.append(jnp.exp(b_last + m_prev - m_new))
        dec_scale.append(jnp.exp(g_max - m_new))
        inter = bcum[rows] + m_prev
        m_r = jnp.maximum(m_loc[rows], inter)
        w_inter.append(jnp.exp(inter - m_r))
        w_intra.append(jnp.exp(m_loc[rows] - m_r))
        floor.append(jnp.exp(-m_r))
        wk_t.append(jnp.exp(b_last - bcum[rows] + ig[rows] - g_max))
        m_prev = m_new
    m_ref[0:1, :] = m_prev
    pad_rows = -2 * n_chunks % SUBLANES
    if pad_rows:
        dec_scale.append(jnp.zeros((pad_rows, 128), F32))
    per_time = [cmax, jnp.concatenate(w_intra, axis=0), jnp.concatenate(w_inter, axis=0),
                jnp.concatenate(floor, axis=0), jnp.concatenate(dec_scale, axis=0)]
    offs = [0]
    for t in per_time:
        offs.append(offs[-1] + t.shape[0])
    for h in range(HEADS):
        for t, o in zip(per_time, offs):
            rep_ref[h, o:o + t.shape[0], :] = jnp.broadcast_to(t[:, h:h + 1], t.shape)

    r_i = lax.broadcasted_iota(jnp.int32, (CHUNK, CHUNK), 0)
    c_i = lax.broadcasted_iota(jnp.int32, (CHUNK, CHUNK), 1)
    causal = c_i <= r_i
    ones = jnp.ones((CHUNK, ML_DH), BF16)
    for c in range(n_chunks):
        rows = slice(CHUNK * c, CHUNK * (c + 1))
        d_t = _time_on_lanes(dgate[rows])
        wk_tt = _time_on_lanes(wk_t[c])
        for h in range(HEADS):
            cols = slice(ML_DH * h, ML_DH * (h + 1))
            rep = lambda i: rep_ref[h, offs[i] + CHUNK * c:offs[i] + CHUNK * (c + 1), :]
            cmax_b, a_b, i_b, floor_b = rep(0), rep(1), rep(2), rep(3)
            dec_b = rep_ref[h, offs[4] + 2 * c:offs[4] + 2 * c + 1, :]
            scale_b = rep_ref[h, offs[4] + 2 * c + 1:offs[4] + 2 * c + 2, :]
            qb = q_ref[rows, cols]
            kt = kt_ref[h, :, rows]
            v_ext = jnp.concatenate(
                [p_ref[rows, GROUP_W + ML_DH * h:GROUP_W + ML_DH * (h + 1)].astype(BF16), ones], axis=1)
            cn_prev = cn_ref[h]

            w = jnp.exp(jnp.where(causal, d_t[h:h + 1, 0:CHUNK] - cmax_b[:, 0:CHUNK], -jnp.inf))
            s = _dot(qb, kt.astype(BF16)) * w
            intra = _dot(s.astype(BF16), v_ext)
            inter = _dot(qb, cn_prev.astype(BF16))
            both = jnp.concatenate([a_b, a_b], axis=1) * intra + jnp.concatenate([i_b, i_b], axis=1) * inter
            hh = both[:, 0:ML_DH] / jnp.maximum(jnp.abs(both[:, ML_DH:]), floor_b)

            ktw = (kt * wk_tt[h:h + 1, 0:CHUNK]).astype(BF16)
            cn_ref[h] = (jnp.concatenate([dec_b, dec_b], axis=1) * cn_prev
                         + jnp.concatenate([scale_b, scale_b], axis=1) * _dot(ktw, v_ext))

            og = p_ref[rows, 2 * GROUP_W + ML_DH * h:2 * GROUP_W + ML_DH * (h + 1)]
            z = p_ref[rows, 3 * GROUP_W + ML_DH * h:3 * GROUP_W + ML_DH * (h + 1)]
            o_ref[rows, cols] = (_norm_rows(hh * _sigmoid(og), GN_EPS) * gn_ref[:, cols] * _silu(z))


def _mlstm(p_ml, p_if, w, bsz, seq):
    conv_w, wq, wkt, bias, gn = w
    t_blk = min(ML_BLOCK_CHUNKS * CHUNK, seq)
    nb = seq // t_blk
    n_chunks = t_blk // CHUNK
    rep_rows = 4 * t_blk + 2 * n_chunks + (-2 * n_chunks % SUBLANES)
    row = lambda b, c: (b * nb + c, 0)
    const2 = lambda b, c: (0, 0)
    const3 = lambda b, c: (0, 0, 0)
    return pl.pallas_call(
        _ml_kernel,
        grid=(bsz, nb),
        in_specs=[pl.BlockSpec((t_blk, W_ML), row),
                  pl.BlockSpec((t_blk, W_IF), row),
                  pl.BlockSpec(conv_w.shape, const2),
                  pl.BlockSpec(wq.shape, const3),
                  pl.BlockSpec(wkt.shape, const3),
                  pl.BlockSpec(bias.shape, const2),
                  pl.BlockSpec(gn.shape, const2)],
        out_specs=pl.BlockSpec((t_blk, GROUP_W), row),
        out_shape=jax.ShapeDtypeStruct((bsz * seq, GROUP_W), F32),
        scratch_shapes=[pltpu.VMEM((t_blk + SUBLANES, GROUP_W), F32),
                        pltpu.VMEM((t_blk, GROUP_W), BF16),
                        pltpu.VMEM((HEADS, ML_DH, t_blk), F32),
                        pltpu.VMEM((HEADS, rep_rows, 128), F32),
                        pltpu.VMEM((HEADS, ML_DH, 2 * ML_DH), F32),
                        pltpu.VMEM((SUBLANES, 128), F32)],
        compiler_params=_params(("parallel", "arbitrary")),
        name="mlstm",
    )(p_ml, p_if, conv_w, wq, wkt, bias, gn)


def _xa_kernel(p_ref, mk_ref, mv_ref, o_ref):
    z = p_ref[:, GROUP_W:]
    outs = []
    for h in range(HEADS):
        cols = slice(XA_DH * h, XA_DH * (h + 1))
        s = _dot_nt(p_ref[:, cols].astype(BF16), mk_ref[:, cols]) * XA_DH ** -0.5
        e = jnp.exp(s - jnp.max(s, axis=1, keepdims=True))
        p = e / jnp.sum(e, axis=1, keepdims=True)
        outs.append(_dot(p.astype(BF16), mv_ref[:, cols]))
    o_ref[...] = jnp.concatenate(outs, axis=1) * _silu(z)


def _cross_attention(p_xa, mk, mv, bsz, seq):
    t_blk = min(256, seq)
    nb = seq // t_blk
    row = lambda b, j: (b * nb + j, 0)
    mem = lambda b, j: (b, 0)
    return pl.pallas_call(
        _xa_kernel,
        grid=(bsz, nb),
        in_specs=[pl.BlockSpec((t_blk, W_XA), row),
                  pl.BlockSpec((N_MEM, GROUP_W), mem),
                  pl.BlockSpec((N_MEM, GROUP_W), mem)],
        out_specs=pl.BlockSpec((t_blk, GROUP_W), row),
        out_shape=jax.ShapeDtypeStruct((bsz * seq, GROUP_W), F32),
        compiler_params=_params(("parallel", "parallel")),
        name="memory_attention",
    )(p_xa, mk, mv)


def _out_kernel(x_ref, r_ref, s_ref, m_ref, a_ref, w_ref, g_ref, b_ref, o_ref):
    y = None
    s5o = jnp.concatenate([s_ref[j] for j in range(s_ref.shape[0])], axis=1)
    for i, t in enumerate((r_ref[...], s5o, m_ref[...], a_ref[...])):
        part = _dot(t.astype(BF16), w_ref[GROUP_W * i:GROUP_W * (i + 1), :])
        y = part if y is None else y + part
    t = ALPHA * x_ref[...] + y
    o_ref[...] = _norm_rows(t, LN_EPS) * g_ref[...] + b_ref[...]


def _out_project(x2, groups, w_out, ln_g, ln_b):
    m = x2.shape[0]
    tm = min(256, m)
    row = lambda i: (i, 0)
    const = lambda i: (0, 0)
    return pl.pallas_call(
        _out_kernel,
        grid=(m // tm,),
        in_specs=[pl.BlockSpec((tm, D_MODEL), row),
                  pl.BlockSpec((tm, GROUP_W), row),
                  pl.BlockSpec((GROUP_W // 128, tm, 128), lambda i: (0, i, 0)),
                  pl.BlockSpec((tm, GROUP_W), row),
                  pl.BlockSpec((tm, GROUP_W), row),
                  pl.BlockSpec(w_out.shape, const),
                  pl.BlockSpec((1, D_MODEL), const),
                  pl.BlockSpec((1, D_MODEL), const)],
        out_specs=pl.BlockSpec((tm, D_MODEL), row),
        out_shape=jax.ShapeDtypeStruct((m, D_MODEL), F32),
        compiler_params=_params(("parallel",)),
        name="out_proj_norm",
    )(x2, *groups, w_out, ln_g, ln_b)


def _reorder_w_in(w_in):
    o = 0
    cols = {}
    for name, w in (("rq", 256), ("rk", 256), ("rv", 512), ("rz", 512), ("su", 512), ("sz", 512),
                    ("mx", 512), ("mv", 512), ("mo", 512), ("mz", 512), ("mi", 4), ("mf", 4),
                    ("aq", 512), ("az", 512)):
        cols[name] = w_in[:, o:o + w]
        o += w
    half = RET_DK // 2
    perm = jnp.array([h * RET_DK + s * half + j for s in range(2) for h in range(HEADS)
                      for j in range(half)], jnp.int32)
    rq = cols["rq"][:, perm]
    rk = cols["rk"][:, perm] * RET_DK ** -0.5
    pad = jnp.zeros((w_in.shape[0], W_IF - 2 * HEADS), w_in.dtype)
    return jnp.concatenate(
        [cols["mx"], cols["mv"], cols["mo"], cols["mz"], cols["su"], cols["sz"],
         cols["aq"], cols["az"], rq, rk, cols["rv"], cols["rz"], cols["mi"], cols["mf"], pad],
        axis=1).astype(BF16)


def _prepare_weights(t_s5, w_in, s5_a_re, s5_a_im, s5_log_dt, s5_b_re, s5_b_im, s5_c_re, s5_c_im,
                     s5_d, s5_w_glu, ml_conv_w, ml_wq, ml_wk, ml_b_ig, ml_b_fg, ml_gn, xa_w_kv, w_out,
                     ln_g, ln_b):
    bias = jnp.concatenate([ml_b_ig, ml_b_fg, jnp.zeros((W_IF - 2 * HEADS,), F32)])[None, :]
    return dict(
        w_in=_reorder_w_in(w_in),
        s5=_s5_weights(s5_a_re, s5_a_im, s5_log_dt, s5_b_re, s5_b_im, s5_c_re, s5_c_im, s5_d,
                       s5_w_glu, t_s5),
        ml=(ml_conv_w, ml_wq.astype(BF16), jnp.swapaxes(ml_wk, 1, 2).astype(BF16), bias,
            ml_gn[None, :]),
        w_kv=xa_w_kv.astype(BF16), w_out=w_out.astype(BF16), ln_g=ln_g[None, :], ln_b=ln_b[None, :])


def _layer(x2, mem2, cos, sin, ret_tabs, t_s5, bsz, seq, w):
    p_ml, p_s5, p_xa, p_ret, p_if = _project(x2, w["w_in"])
    ret = _retention(p_ret, cos, sin, ret_tabs, bsz, seq)
    s5o = _s5(p_s5, w["s5"], t_s5, bsz, seq)
    ml = _mlstm(p_ml, p_if, w["ml"], bsz, seq)
    mk, mv = _memory_kv(mem2, w["w_kv"])
    xa = _cross_attention(p_xa, mk, mv, bsz, seq)
    return _out_project(x2, (ret, s5o, ml, xa), w["w_out"], w["ln_g"], w["ln_b"])


def kernel(x, mem, positions, w_in, s5_a_re, s5_a_im, s5_log_dt, s5_b_re, s5_b_im, s5_c_re, s5_c_im,
           s5_d, s5_w_glu, ml_conv_w, ml_wq, ml_wk, ml_b_ig, ml_b_fg, ml_gn, xa_w_kv, w_out, ln_g,
           ln_b):
    bsz, seq, _ = x.shape
    x2 = x.reshape(bsz * seq, D_MODEL)
    mem2 = mem.reshape(bsz * N_MEM, D_MODEL)
    cos, sin = _rope_tables(positions)
    ret_tabs = _retention_tables()
    t_s5 = _s5_block_tokens(seq)
    weights = jax.vmap(functools.partial(_prepare_weights, t_s5))(
        w_in, s5_a_re, s5_a_im, s5_log_dt, s5_b_re, s5_b_im, s5_c_re, s5_c_im, s5_d, s5_w_glu,
        ml_conv_w, ml_wq, ml_wk, ml_b_ig, ml_b_fg, ml_gn, xa_w_kv, w_out, ln_g, ln_b)
    for l in range(w_in.shape[0]):
        x2 = _layer(x2, mem2, cos, sin, ret_tabs, t_s5, bsz, seq,
                    jax.tree.map(lambda t: t[l], weights))
    return x2.reshape(bsz, seq, D_MODEL)
```

```python
import functools
import math

import jax
import jax.numpy as jnp
from jax import lax
from jax.experimental import pallas as pl
from jax.experimental.pallas import tpu as pltpu

F32 = jnp.float32
BF16 = jnp.bfloat16

D_MODEL = 1024
CHUNK = 64
N_MEM = 256
GROUP_W = 512
HEADS = 4
RET_DK = 64
RET_DV = 128
ROPE_BASE = 10000.0
S5_CH = 16
S5_GROUPS = 32
S5_STATE = 64
S5_LANES = S5_GROUPS * S5_STATE
ML_DH = 128
ML_CONV = 4
XA_DH = 128
LN_EPS = 1e-5
GN_EPS = 1e-6
DEPTH = 4
ALPHA = (2 * DEPTH) ** 0.25

SUBLANES = 8
VMEM_LIMIT = 56 * 1024 * 1024

W_ML, W_S5, W_XA, W_RET, W_IF = 4 * GROUP_W, 2 * GROUP_W, 2 * GROUP_W, 3 * GROUP_W, 128


def _params(sem):
    return pltpu.CompilerParams(dimension_semantics=sem, vmem_limit_bytes=VMEM_LIMIT)


def _sigmoid(x):
    return 1.0 / (1.0 + jnp.exp(-x))


def _silu(x):
    return x * _sigmoid(x)


def _norm_rows(t, eps):
    mu = jnp.mean(t, axis=-1, keepdims=True)
    d = t - mu
    var = jnp.mean(d * d, axis=-1, keepdims=True)
    return d * lax.rsqrt(var + eps)


def _dot(a, b):
    return jnp.dot(a, b, preferred_element_type=F32)


def _dot_nt(a, b):
    return lax.dot_general(a, b, (((1,), (1,)), ((), ())), preferred_element_type=F32)


def _dot_tn(a, b):
    return lax.dot_general(a, b, (((0,), (0,)), ((), ())), preferred_element_type=F32)


def _rope_kernel(pos_ref, inv_ref, cos_ref, sin_ref):
    ang = pos_ref[...].astype(F32) * inv_ref[...]
    cos_ref[...] = jnp.cos(ang)
    sin_ref[...] = jnp.sin(ang)


def _rope_tables(positions):
    m = positions.size
    half = RET_DK // 2
    inv = ROPE_BASE ** (-jnp.arange(half, dtype=F32) / half)
    inv = jnp.tile(inv, HEADS)[None, :]
    tm = min(512, m)
    return pl.pallas_call(
        _rope_kernel,
        grid=(m // tm,),
        in_specs=[pl.BlockSpec((tm, 1), lambda i: (i, 0)),
                  pl.BlockSpec((1, 128), lambda i: (0, 0))],
        out_specs=[pl.BlockSpec((tm, 128), lambda i: (i, 0))] * 2,
        out_shape=[jax.ShapeDtypeStruct((m, 128), F32)] * 2,
        compiler_params=_params(("parallel",)),
        name="rope_tables",
    )(positions.reshape(m, 1), inv)


PROJ_ROWS = 512
S5_TIME_BLOCK = CHUNK

C_RV, C_SU, C_MX, C_MI, C_AQ = 512, 1536, 2560, 4608, 4616


def _proj_kernel(x_ref, w_ref, wqk_ref, wxa_ref, wif_ref, o_ml, o_s5, o_xa, o_ret, o_if):
    xb = x_ref[...].astype(BF16)
    o_ml[...] = _dot(xb, w_ref[:, C_MX:C_MX + W_ML])
    tb = S5_TIME_BLOCK
    for j in range(W_S5 // 128):
        r = _dot(xb, w_ref[:, C_SU + 128 * j:C_SU + 128 * (j + 1)])
        for t in range(o_s5.shape[1]):
            o_s5[j, t, 0] = r[tb * t:tb * (t + 1), :]
    o_xa[...] = _dot(xb, wxa_ref[...])
    o_ret[:, 0:2 * HEADS * RET_DK] = _dot(xb, wqk_ref[...])
    o_ret[:, 2 * HEADS * RET_DK:] = _dot(xb, w_ref[:, C_RV:C_SU])
    o_if[...] = _dot(xb, wif_ref[...])


def _project(x2, w, bsz, seq):
    w_main, w_qk, w_xa, w_if = w
    m = x2.shape[0]
    tm = min(PROJ_ROWS, seq)
    nj = seq // tm
    tpb = tm // S5_TIME_BLOCK
    row = lambda i: (i, 0)
    const = lambda i: (0, 0)
    s5_shape = (W_S5 // 128, seq // S5_TIME_BLOCK, bsz, S5_TIME_BLOCK, 128)
    specs = [pl.BlockSpec((tm, W_ML), row),
             pl.BlockSpec((W_S5 // 128, tpb, 1, S5_TIME_BLOCK, 128),
                          lambda i: (0, i % nj, i // nj, 0, 0)),
             pl.BlockSpec((tm, W_XA), row),
             pl.BlockSpec((tm, W_RET), row),
             pl.BlockSpec((tm, W_IF), row)]
    shapes = [(m, W_ML), s5_shape, (m, W_XA), (m, W_RET), (m, W_IF)]
    return pl.pallas_call(
        _proj_kernel,
        grid=(m // tm,),
        in_specs=[pl.BlockSpec((tm, D_MODEL), row)] +
                 [pl.BlockSpec(t.shape, const, pipeline_mode=pl.Buffered(1))
                  for t in (w_main, w_qk, w_xa, w_if)],
        out_specs=specs,
        out_shape=[jax.ShapeDtypeStruct(sh, F32) for sh in shapes],
        compiler_params=_params(("parallel",)),
        name="in_proj",
    )(x2, w_main, w_qk, w_xa, w_if)


def _kv_kernel(m_ref, w_ref, k_ref, v_ref):
    mb = m_ref[...].astype(BF16)
    k_ref[...] = _dot(mb, w_ref[:, :GROUP_W]).astype(BF16)
    v_ref[...] = _dot(mb, w_ref[:, GROUP_W:]).astype(BF16)


def _memory_kv(mem2, w_kv):
    m = mem2.shape[0]
    return pl.pallas_call(
        _kv_kernel,
        grid=(m // N_MEM,),
        in_specs=[pl.BlockSpec((N_MEM, D_MODEL), lambda i: (i, 0)),
                  pl.BlockSpec(w_kv.shape, lambda i: (0, 0))],
        out_specs=[pl.BlockSpec((N_MEM, GROUP_W), lambda i: (i, 0))] * 2,
        out_shape=[jax.ShapeDtypeStruct((m, GROUP_W), BF16)] * 2,
        compiler_params=_params(("parallel",)),
        name="memory_kv",
    )(mem2, w_kv)


RET_BLOCK_CHUNKS = 4


def _ret_kernel(p_ref, cos_ref, sin_ref, hmask_ref, intra_ref, qdec_ref, kdec_ref, cdec_ref,
                o_ref, s_ref):
    @pl.when(pl.program_id(1) == 0)
    def _():
        s_ref[...] = jnp.zeros_like(s_ref)

    cs = cos_ref[...]
    sn = sin_ref[...]

    def rot(t):
        t1 = t[:, :128]
        t2 = t[:, 128:]
        return jnp.concatenate([t1 * cs - t2 * sn, t1 * sn + t2 * cs], axis=1)

    qr = rot(p_ref[:, 0:256])
    kr = rot(p_ref[:, 256:512])
    for c in range(p_ref.shape[0] // CHUNK):
        rows = slice(CHUNK * c, CHUNK * (c + 1))
        vb = p_ref[rows, 512:1024].astype(BF16)
        krb = kr[rows].astype(BF16)
        state = s_ref[...]
        sb = state.astype(BF16)
        for h in range(HEADS):
            cols = slice(RET_DV * h, RET_DV * (h + 1))
            qm = (qr[rows] * hmask_ref[h:h + 1, :]).astype(BF16)
            att = (_dot_nt(qm, krb) * intra_ref[h]).astype(BF16)
            o = _dot(att, vb[:, cols]) + qdec_ref[:, cols] * _dot(qm, sb[:, cols])
            z = p_ref[rows, 1024 + RET_DV * h:1024 + RET_DV * (h + 1)]
            o_ref[rows, cols] = _norm_rows(o, GN_EPS) * _silu(z)
        kd = (kr[rows] * kdec_ref[...]).astype(BF16)
        s_ref[...] = cdec_ref[...] * state + _dot_tn(kd, vb)


def _retention(p_ret, cos, sin, tabs, bsz, seq):
    t_blk = min(RET_BLOCK_CHUNKS * CHUNK, seq)
    nb = seq // t_blk
    row = lambda b, c: (b * nb + c, 0)
    const2 = lambda b, c: (0, 0)
    hmask, intra, qdec, kdec, cdec = tabs
    return pl.pallas_call(
        _ret_kernel,
        grid=(bsz, nb),
        in_specs=[pl.BlockSpec((t_blk, W_RET), row),
                  pl.BlockSpec((t_blk, 128), row),
                  pl.BlockSpec((t_blk, 128), row),
                  pl.BlockSpec(hmask.shape, const2),
                  pl.BlockSpec(intra.shape, lambda b, c: (0, 0, 0)),
                  pl.BlockSpec(qdec.shape, const2),
                  pl.BlockSpec(kdec.shape, const2),
                  pl.BlockSpec(cdec.shape, const2)],
        out_specs=pl.BlockSpec((t_blk, GROUP_W), row),
        out_shape=jax.ShapeDtypeStruct((bsz * seq, GROUP_W), F32),
        scratch_shapes=[pltpu.VMEM((HEADS * RET_DK, HEADS * RET_DV), F32)],
        compiler_params=_params(("parallel", "arbitrary")),
        name="retention",
    )(p_ret, cos, sin, hmask, intra, qdec, kdec, cdec)


def _retention_tables():
    log_g = jnp.log1p(-jnp.exp2(-5.0 - jnp.arange(HEADS, dtype=F32)))
    r = jnp.arange(CHUNK, dtype=F32)
    intra = jnp.exp(jnp.abs(r[:, None] - r[None, :]) * log_g[:, None, None])
    q_dec = jnp.exp((r + 1.0) * log_g[:, None])
    k_dec = jnp.exp((CHUNK - 1.0 - r) * log_g[:, None])
    c_dec = jnp.exp(CHUNK * log_g)
    lane_head = (jnp.arange(HEADS * RET_DK) % 128) // (RET_DK // 2)
    hmask = (lane_head[None, :] == jnp.arange(HEADS)[:, None]).astype(F32)
    qdec = jnp.repeat(q_dec.T, RET_DV, axis=1)
    kdec = k_dec.T[:, lane_head]
    cdec = jnp.repeat(c_dec, RET_DV)[None, :]
    return hmask, intra, qdec, kdec, cdec


def _gelu_tanh(x):
    return 0.5 * x * (1.0 + jnp.tanh(math.sqrt(2.0 / math.pi) * (x + 0.044715 * (x * x * x))))


S5_LANE_BLOCK = 512
S5_KTILE_GROUPS = 16
S5_KTILE_IN = S5_KTILE_GROUPS * S5_CH
S5_KTILE_LANES = S5_KTILE_GROUPS * S5_STATE
S5_KTILES = S5_GROUPS // S5_KTILE_GROUPS


def _s5_kernel(p_ref, bbd_ref, cbd_ref, a_ref, d_ref, wglu_ref, o_ref, bu_ref, carry_ref):
    @pl.when(pl.program_id(0) == 0)
    def _():
        carry_ref[...] = jnp.zeros_like(carry_ref)

    tb = S5_TIME_BLOCK

    def gather(j):
        return jnp.concatenate([p_ref[j, 0, pl.ds(s, SUBLANES, stride=tb), :] for s in range(tb)],
                               axis=0)

    nblk = GROUP_W // 128
    u = jnp.concatenate([gather(j) for j in range(nblk)], axis=1)
    z = jnp.concatenate([gather(nblk + j) for j in range(nblk)], axis=1)
    ub = u.astype(BF16)
    kw = 2 * S5_KTILE_LANES
    for kt in range(S5_KTILES):
        bu_ref[:, kt * kw:(kt + 1) * kw] = _dot(ub[:, kt * S5_KTILE_IN:(kt + 1) * S5_KTILE_IN],
                                                bbd_ref[kt])

    lb = S5_LANE_BLOCK
    for kt in range(S5_KTILES):
        for blk in range(S5_KTILE_LANES // lb):
            re = slice(kt * kw + blk * lb, kt * kw + (blk + 1) * lb)
            im = slice(kt * kw + S5_KTILE_LANES + blk * lb, kt * kw + S5_KTILE_LANES + (blk + 1) * lb)
            par = slice(kt * S5_KTILE_LANES + blk * lb, kt * S5_KTILE_LANES + (blk + 1) * lb)
            ar = jnp.broadcast_to(a_ref[0:1, par], (SUBLANES, lb))
            ai = jnp.broadcast_to(a_ref[1:2, par], (SUBLANES, lb))

            def step(s, carry):
                xr, xi = carry
                rows = pl.ds(pl.multiple_of(s * SUBLANES, SUBLANES), SUBLANES)
                nr = ar * xr - ai * xi + bu_ref[rows, re]
                ni = ar * xi + ai * xr + bu_ref[rows, im]
                bu_ref[rows, re] = nr
                bu_ref[rows, im] = ni
                return nr, ni

            xr, xi = lax.fori_loop(0, tb, step, (carry_ref[0, :, par], carry_ref[1, :, par]), unroll=8)
            carry_ref[0, :, par] = xr
            carry_ref[1, :, par] = xi

    y = jnp.concatenate([_dot(bu_ref[:, kt * kw:(kt + 1) * kw].astype(BF16), cbd_ref[kt])
                         for kt in range(S5_KTILES)], axis=1) + d_ref[...] * u
    y = _gelu_tanh(y)
    y = y * _sigmoid(_dot(y.astype(BF16), wglu_ref[...]))
    res = y * _silu(z)
    for j in range(nblk):
        for s in range(tb):
            o_ref[j, 0, pl.ds(s, SUBLANES, stride=tb), :] = (
                res[s * SUBLANES:(s + 1) * SUBLANES, 128 * j:128 * (j + 1)])


def _s5(p_s5, w, bsz, seq):
    assert bsz == SUBLANES, "the S5 kernel puts one batch element on each sublane"
    nb = seq // S5_TIME_BLOCK
    rows = bsz * S5_TIME_BLOCK
    blk = lambda j: (0, j, 0, 0)
    out = pl.pallas_call(
        _s5_kernel,
        grid=(nb,),
        in_specs=[pl.BlockSpec((W_S5 // 128, 1, rows, 128), blk)] +
                 [pl.BlockSpec(t.shape, lambda j, nd=t.ndim: (0,) * nd) for t in w],
        out_specs=pl.BlockSpec((GROUP_W // 128, 1, rows, 128), blk),
        out_shape=jax.ShapeDtypeStruct((GROUP_W // 128, nb, rows, 128), F32),
        scratch_shapes=[pltpu.VMEM((rows, 2 * S5_LANES), F32),
                        pltpu.VMEM((2, SUBLANES, S5_LANES), F32)],
        compiler_params=_params(("arbitrary",)),
        name="s5",
    )(p_s5.reshape(W_S5 // 128, nb, rows, 128), *w)
    return out.reshape(GROUP_W // 128, nb, bsz, S5_TIME_BLOCK, 128)


def _s5_weights(a_re, a_im, log_dt, b_re, b_im, c_re, c_im, d, w_glu):
    dt = jnp.exp(log_dt)[:, None]
    mag = jnp.exp(a_re * dt)
    ab_re = mag * jnp.cos(a_im * dt)
    ab_im = mag * jnp.sin(a_im * dt)
    den = a_re * a_re + a_im * a_im
    p = ab_re - 1.0
    f_re = (p * a_re + ab_im * a_im) / den
    f_im = (ab_im * a_re - p * a_im) / den
    bb_re = f_re[..., None] * b_re - f_im[..., None] * b_im
    bb_im = f_re[..., None] * b_im + f_im[..., None] * b_re
    kg = S5_KTILE_GROUPS
    own_group = (jnp.arange(kg)[:, None, None, None] == jnp.arange(kg)[None, None, :, None])

    def tiles_in(t):
        t = jnp.swapaxes(t, 1, 2).reshape(S5_KTILES, kg, S5_CH, 1, S5_STATE)
        return jnp.where(own_group, t, 0.0).reshape(S5_KTILES, S5_KTILE_IN, S5_KTILE_LANES)

    def tiles_out(t):
        t = jnp.swapaxes(t, 1, 2).reshape(S5_KTILES, kg, S5_STATE, 1, S5_CH)
        return jnp.where(own_group, t, 0.0).reshape(S5_KTILES, S5_KTILE_LANES, S5_KTILE_IN)

    bbd = jnp.concatenate([tiles_in(bb_re), tiles_in(bb_im)], axis=2).astype(BF16)
    cbd = jnp.concatenate([tiles_out(c_re), tiles_out(-c_im)], axis=1).astype(BF16)
    a = jnp.stack([ab_re.reshape(-1), ab_im.reshape(-1)])
    return bbd, cbd, a, d[None, :], w_glu.astype(BF16)


def _log_sigmoid(x):
    return jnp.minimum(x, 0.0) - jnp.log1p(jnp.exp(-jnp.abs(x)))


ML_BLOCK_CHUNKS = 4


def _segmented_scan(x, row_in_chunk, combine, identity):
    sh = 1
    while sh < CHUNK:
        x = combine(x, jnp.where(row_in_chunk >= sh, pltpu.roll(x, sh, 0), identity))
        sh *= 2
    return x


def _time_on_lanes(t):
    return jnp.concatenate([t, jnp.zeros_like(t)], axis=0).T


def _ml_kernel(p_ref, g_ref, conv_ref, wq_ref, wkt_ref, bias_ref, gn_ref, o_ref,
               xbuf_ref, q_ref, kt_ref, rep_ref, cn_ref, m_ref):
    t_blk = p_ref.shape[0]
    n_chunks = t_blk // CHUNK

    @pl.when(pl.program_id(1) == 0)
    def _():
        xbuf_ref[0:SUBLANES, :] = jnp.zeros((SUBLANES, GROUP_W), F32)
        cn_ref[...] = jnp.zeros_like(cn_ref)
        m_ref[...] = jnp.zeros_like(m_ref)

    xbuf_ref[SUBLANES:, :] = p_ref[:, 0:GROUP_W]
    conv = jnp.zeros((t_blk, GROUP_W), F32)
    for j in range(ML_CONV):
        off = SUBLANES - (ML_CONV - 1) + j
        conv = conv + conv_ref[j:j + 1, :] * xbuf_ref[off:off + t_blk, :]
    xbuf_ref[0:SUBLANES, :] = xbuf_ref[t_blk:t_blk + SUBLANES, :]
    xc = _silu(conv).astype(BF16)
    for h in range(HEADS):
        cols = slice(ML_DH * h, ML_DH * (h + 1))
        q_ref[:, cols] = _dot(xc[:, cols], wq_ref[h]).astype(BF16)
        kt_ref[h] = _dot_nt(wkt_ref[h], xc[:, cols]) * ML_DH ** -0.5

    gates = g_ref[...] + bias_ref[...]
    ig = gates
    lf = _log_sigmoid(pltpu.roll(gates, 128 - HEADS, 1))
    row_in_chunk = lax.broadcasted_iota(jnp.int32, (t_blk, 128), 0) % CHUNK
    bcum = _segmented_scan(lf, row_in_chunk, jnp.add, 0.0)
    dgate = ig - bcum
    cmax = _segmented_scan(dgate, row_in_chunk, jnp.maximum, -jnp.inf)
    m_loc = bcum + cmax
    m_prev = m_ref[0:1, :]
    w_intra, w_inter, floor, wk_t, dec_scale = [], [], [], [], []
    for c in range(n_chunks):
        rows = slice(CHUNK * c, CHUNK * (c + 1))
        last = slice(CHUNK * (c + 1) - 1, CHUNK * (c + 1))
        b_last = bcum[last]
        g_max = b_last + cmax[last]
        m_new = jnp.maximum(b_last + m_prev, g_max)
        dec_scale.append(jnp.exp(b_last + m_prev - m_new))
        dec_scale.append(jnp.exp(g_max - m_new))
        inter = bcum[rows] + m_prev
        m_r = jnp.maximum(m_loc[rows], inter)
        w_inter.append(jnp.exp(inter - m_r))
        w_intra.append(jnp.exp(m_loc[rows] - m_r))
        floor.append(jnp.exp(-m_r))
        wk_t.append(jnp.exp(b_last - bcum[rows] + ig[rows] - g_max))
        m_prev = m_new
    m_ref[0:1, :] = m_prev
    pad_rows = -2 * n_chunks % SUBLANES
    if pad_rows:
        dec_scale.append(jnp.zeros((pad_rows, 128), F32))
    per_time = [cmax, jnp.concatenate(w_intra, axis=0), jnp.concatenate(w_inter, axis=0),
                jnp.concatenate(floor, axis=0), jnp.concatenate(dec_scale, axis=0)]
    offs = [0]
    for t in per_time:
        offs.append(offs[-1] + t.shape[0])
    for h in range(HEADS):
        for t, o in zip(per_time, offs):
            rep_ref[h, o:o + t.shape[0], :] = jnp.broadcast_to(t[:, h:h + 1], t.shape)

    r_i = lax.broadcasted_iota(jnp.int32, (CHUNK, CHUNK), 0)
    c_i = lax.broadcasted_iota(jnp.int32, (CHUNK, CHUNK), 1)
    causal = c_i <= r_i
    ones = jnp.ones((CHUNK, ML_DH), BF16)
    for c in range(n_chunks):
        rows = slice(CHUNK * c, CHUNK * (c + 1))
        d_t = _time_on_lanes(dgate[rows])
        wk_tt = _time_on_lanes(wk_t[c])
        for h in range(HEADS):
            cols = slice(ML_DH * h, ML_DH * (h + 1))
            rep = lambda i: rep_ref[h, offs[i] + CHUNK * c:offs[i] + CHUNK * (c + 1), :]
            cmax_b, a_b, i_b, floor_b = rep(0), rep(1), rep(2), rep(3)
            dec_b = rep_ref[h, offs[4] + 2 * c:offs[4] + 2 * c + 1, :]
            scale_b = rep_ref[h, offs[4] + 2 * c + 1:offs[4] + 2 * c + 2, :]
            qb = q_ref[rows, cols]
            kt = kt_ref[h, :, rows]
            v_ext = jnp.concatenate(
                [p_ref[rows, GROUP_W + ML_DH * h:GROUP_W + ML_DH * (h + 1)].astype(BF16), ones], axis=1)
            cn_prev = cn_ref[h]

            w = jnp.exp(jnp.where(causal, d_t[h:h + 1, 0:CHUNK] - cmax_b[:, 0:CHUNK], -jnp.inf))
            s = _dot(qb, kt.astype(BF16)) * w
            intra = _dot(s.astype(BF16), v_ext)
            inter = _dot(qb, cn_prev.astype(BF16))
            both = jnp.concatenate([a_b, a_b], axis=1) * intra + jnp.concatenate([i_b, i_b], axis=1) * inter
            hh = both[:, 0:ML_DH] / jnp.maximum(jnp.abs(both[:, ML_DH:]), floor_b)

            ktw = (kt * wk_tt[h:h + 1, 0:CHUNK]).astype(BF16)
            cn_ref[h] = (jnp.concatenate([dec_b, dec_b], axis=1) * cn_prev
                         + jnp.concatenate([scale_b, scale_b], axis=1) * _dot(ktw, v_ext))

            og = p_ref[rows, 2 * GROUP_W + ML_DH * h:2 * GROUP_W + ML_DH * (h + 1)]
            z = p_ref[rows, 3 * GROUP_W + ML_DH * h:3 * GROUP_W + ML_DH * (h + 1)]
            o_ref[rows, cols] = (_norm_rows(hh * _sigmoid(og), GN_EPS) * gn_ref[:, cols] * _silu(z))


def _mlstm(p_ml, p_if, w, bsz, seq):
    conv_w, wq, wkt, bias, gn = w
    t_blk = min(ML_BLOCK_CHUNKS * CHUNK, seq)
    nb = seq // t_blk
    n_chunks = t_blk // CHUNK
    rep_rows = 4 * t_blk + 2 * n_chunks + (-2 * n_chunks % SUBLANES)
    row = lambda b, c: (b * nb + c, 0)
    const2 = lambda b, c: (0, 0)
    const3 = lambda b, c: (0, 0, 0)
    return pl.pallas_call(
        _ml_kernel,
        grid=(bsz, nb),
        in_specs=[pl.BlockSpec((t_blk, W_ML), row),
                  pl.BlockSpec((t_blk, W_IF), row),
                  pl.BlockSpec(conv_w.shape, const2),
                  pl.BlockSpec(wq.shape, const3),
                  pl.BlockSpec(wkt.shape, const3),
                  pl.BlockSpec(bias.shape, const2),
                  pl.BlockSpec(gn.shape, const2)],
        out_specs=pl.BlockSpec((t_blk, GROUP_W), row),
        out_shape=jax.ShapeDtypeStruct((bsz * seq, GROUP_W), F32),
        scratch_shapes=[pltpu.VMEM((t_blk + SUBLANES, GROUP_W), F32),
                        pltpu.VMEM((t_blk, GROUP_W), BF16),
                        pltpu.VMEM((HEADS, ML_DH, t_blk), F32),
                        pltpu.VMEM((HEADS, rep_rows, 128), F32),
                        pltpu.VMEM((HEADS, ML_DH, 2 * ML_DH), F32),
                        pltpu.VMEM((SUBLANES, 128), F32)],
        compiler_params=_params(("parallel", "arbitrary")),
        name="mlstm",
    )(p_ml, p_if, conv_w, wq, wkt, bias, gn)


def _xa_kernel(p_ref, mk_ref, mv_ref, o_ref):
    z = p_ref[:, GROUP_W:]
    outs = []
    for h in range(HEADS):
        cols = slice(XA_DH * h, XA_DH * (h + 1))
        s = _dot_nt(p_ref[:, cols].astype(BF16), mk_ref[:, cols]) * XA_DH ** -0.5
        e = jnp.exp(s - jnp.max(s, axis=1, keepdims=True))
        p = e / jnp.sum(e, axis=1, keepdims=True)
        outs.append(_dot(p.astype(BF16), mv_ref[:, cols]))
    o_ref[...] = jnp.concatenate(outs, axis=1) * _silu(z)


def _cross_attention(p_xa, mk, mv, bsz, seq):
    t_blk = min(256, seq)
    nb = seq // t_blk
    row = lambda b, j: (b * nb + j, 0)
    mem = lambda b, j: (b, 0)
    return pl.pallas_call(
        _xa_kernel,
        grid=(bsz, nb),
        in_specs=[pl.BlockSpec((t_blk, W_XA), row),
                  pl.BlockSpec((N_MEM, GROUP_W), mem),
                  pl.BlockSpec((N_MEM, GROUP_W), mem)],
        out_specs=pl.BlockSpec((t_blk, GROUP_W), row),
        out_shape=jax.ShapeDtypeStruct((bsz * seq, GROUP_W), F32),
        compiler_params=_params(("parallel", "parallel")),
        name="memory_attention",
    )(p_xa, mk, mv)


def _out_kernel(x_ref, r_ref, s_ref, m_ref, a_ref, w_ref, g_ref, b_ref, o_ref):
    y = None
    s5o = jnp.concatenate(
        [jnp.concatenate([s_ref[j, t, 0] for t in range(s_ref.shape[1])], axis=0)
         for j in range(s_ref.shape[0])], axis=1)
    for i, t in enumerate((r_ref[...], s5o, m_ref[...], a_ref[...])):
        part = _dot(t.astype(BF16), w_ref[GROUP_W * i:GROUP_W * (i + 1), :])
        y = part if y is None else y + part
    t = ALPHA * x_ref[...] + y
    o_ref[...] = _norm_rows(t, LN_EPS) * g_ref[...] + b_ref[...]


def _out_project(x2, groups, w_out, ln_g, ln_b, seq):
    m = x2.shape[0]
    tm = min(PROJ_ROWS, seq)
    nj = seq // tm
    row = lambda i: (i, 0)
    const = lambda i: (0, 0)
    return pl.pallas_call(
        _out_kernel,
        grid=(m // tm,),
        in_specs=[pl.BlockSpec((tm, D_MODEL), row),
                  pl.BlockSpec((tm, GROUP_W), row),
                  pl.BlockSpec((GROUP_W // 128, tm // S5_TIME_BLOCK, 1, S5_TIME_BLOCK, 128),
                               lambda i: (0, i % nj, i // nj, 0, 0)),
                  pl.BlockSpec((tm, GROUP_W), row),
                  pl.BlockSpec((tm, GROUP_W), row),
                  pl.BlockSpec(w_out.shape, const),
                  pl.BlockSpec((1, D_MODEL), const),
                  pl.BlockSpec((1, D_MODEL), const)],
        out_specs=pl.BlockSpec((tm, D_MODEL), row),
        out_shape=jax.ShapeDtypeStruct((m, D_MODEL), F32),
        compiler_params=_params(("parallel",)),
        name="out_proj_norm",
    )(x2, *groups, w_out, ln_g, ln_b)


def _split_w_in(w_in):
    half = RET_DK // 2
    perm = jnp.array([h * RET_DK + s * half + j for s in range(2) for h in range(HEADS)
                      for j in range(half)], jnp.int32)
    n_qk = HEADS * RET_DK
    w_qk = jnp.concatenate([w_in[:, :n_qk][:, perm],
                            w_in[:, n_qk:2 * n_qk][:, perm] * RET_DK ** -0.5],
                           axis=1).astype(BF16)
    w_xa = w_in[:, C_AQ:C_AQ + W_XA].astype(BF16)
    w_if = jnp.pad(w_in[:, C_MI:C_AQ], ((0, 0), (0, W_IF - 2 * HEADS))).astype(BF16)
    return w_in.astype(BF16), w_qk, w_xa, w_if


def _prepare_weights(w_in, s5_a_re, s5_a_im, s5_log_dt, s5_b_re, s5_b_im, s5_c_re, s5_c_im,
                     s5_d, s5_w_glu, ml_conv_w, ml_wq, ml_wk, ml_b_ig, ml_b_fg, ml_gn, xa_w_kv, w_out,
                     ln_g, ln_b):
    bias = jnp.concatenate([ml_b_ig, ml_b_fg, jnp.zeros((W_IF - 2 * HEADS,), F32)])[None, :]
    return dict(
        w_in=_split_w_in(w_in),
        s5=_s5_weights(s5_a_re, s5_a_im, s5_log_dt, s5_b_re, s5_b_im, s5_c_re, s5_c_im, s5_d,
                       s5_w_glu),
        ml=(ml_conv_w, ml_wq.astype(BF16), jnp.swapaxes(ml_wk, 1, 2).astype(BF16), bias,
            ml_gn[None, :]),
        w_kv=xa_w_kv.astype(BF16), w_out=w_out.astype(BF16), ln_g=ln_g[None, :], ln_b=ln_b[None, :])


def _layer(x2, mem2, cos, sin, ret_tabs, bsz, seq, w):
    p_ml, p_s5, p_xa, p_ret, p_if = _project(x2, w["w_in"], bsz, seq)
    ret = _retention(p_ret, cos, sin, ret_tabs, bsz, seq)
    s5o = _s5(p_s5, w["s5"], bsz, seq)
    ml = _mlstm(p_ml, p_if, w["ml"], bsz, seq)
    mk, mv = _memory_kv(mem2, w["w_kv"])
    xa = _cross_attention(p_xa, mk, mv, bsz, seq)
    return _out_project(x2, (ret, s5o, ml, xa), w["w_out"], w["ln_g"], w["ln_b"], seq)


def kernel(x, mem, positions, w_in, s5_a_re, s5_a_im, s5_log_dt, s5_b_re, s5_b_im, s5_c_re, s5_c_im,
           s5_d, s5_w_glu, ml_conv_w, ml_wq, ml_wk, ml_b_ig, ml_b_fg, ml_gn, xa_w_kv, w_out, ln_g,
           ln_b):
    bsz, seq, _ = x.shape
    x2 = x.reshape(bsz * seq, D_MODEL)
    mem2 = mem.reshape(bsz * N_MEM, D_MODEL)
    cos, sin = _rope_tables(positions)
    ret_tabs = _retention_tables()
    weights = jax.vmap(_prepare_weights)(
        w_in, s5_a_re, s5_a_im, s5_log_dt, s5_b_re, s5_b_im, s5_c_re, s5_c_im, s5_d, s5_w_glu,
        ml_conv_w, ml_wq, ml_wk, ml_b_ig, ml_b_fg, ml_gn, xa_w_kv, w_out, ln_g, ln_b)
    for l in range(w_in.shape[0]):
        x2 = _layer(x2, mem2, cos, sin, ret_tabs, bsz, seq,
                    jax.tree.map(lambda t: t[l], weights))
    return x2.reshape(bsz, seq, D_MODEL)
```

```python
import functools
import math

import jax
import jax.numpy as jnp
from jax import lax
from jax.experimental import pallas as pl
from jax.experimental.pallas import tpu as pltpu

F32 = jnp.float32
BF16 = jnp.bfloat16

D_MODEL = 1024
CHUNK = 64
N_MEM = 256
GROUP_W = 512
HEADS = 4
RET_DK = 64
RET_DV = 128
ROPE_BASE = 10000.0
S5_CH = 16
S5_GROUPS = 32
S5_STATE = 64
S5_LANES = S5_GROUPS * S5_STATE
ML_DH = 128
ML_CONV = 4
XA_DH = 128
LN_EPS = 1e-5
GN_EPS = 1e-6
DEPTH = 4
ALPHA = (2 * DEPTH) ** 0.25

SUBLANES = 8
VMEM_LIMIT = 56 * 1024 * 1024

W_ML, W_S5, W_XA, W_RET, W_IF = 4 * GROUP_W, 2 * GROUP_W, 2 * GROUP_W, 3 * GROUP_W, 128


def _params(sem):
    return pltpu.CompilerParams(dimension_semantics=sem, vmem_limit_bytes=VMEM_LIMIT)


def _sigmoid(x):
    return 1.0 / (1.0 + jnp.exp(-x))


def _silu(x):
    return x * _sigmoid(x)


def _norm_rows(t, eps):
    mu = jnp.mean(t, axis=-1, keepdims=True)
    d = t - mu
    var = jnp.mean(d * d, axis=-1, keepdims=True)
    return d * lax.rsqrt(var + eps)


def _layer_spec(t, layer, **kw):
    return pl.BlockSpec((None,) + t.shape[1:], lambda *_: (layer,) + (0,) * (t.ndim - 1), **kw)


def _dot(a, b):
    return jnp.dot(a, b, preferred_element_type=F32)


def _dot_nt(a, b):
    return lax.dot_general(a, b, (((1,), (1,)), ((), ())), preferred_element_type=F32)


def _dot_tn(a, b):
    return lax.dot_general(a, b, (((0,), (0,)), ((), ())), preferred_element_type=F32)


def _rope_kernel(pos_ref, inv_ref, cos_ref, sin_ref):
    ang = pos_ref[...].astype(F32) * inv_ref[...]
    cos_ref[...] = jnp.cos(ang)
    sin_ref[...] = jnp.sin(ang)


def _rope_tables(positions):
    m = positions.size
    half = RET_DK // 2
    inv = ROPE_BASE ** (-jnp.arange(half, dtype=F32) / half)
    inv = jnp.tile(inv, HEADS)[None, :]
    tm = min(512, m)
    return pl.pallas_call(
        _rope_kernel,
        grid=(m // tm,),
        in_specs=[pl.BlockSpec((tm, 1), lambda i: (i, 0)),
                  pl.BlockSpec((1, 128), lambda i: (0, 0))],
        out_specs=[pl.BlockSpec((tm, 128), lambda i: (i, 0))] * 2,
        out_shape=[jax.ShapeDtypeStruct((m, 128), F32)] * 2,
        compiler_params=_params(("parallel",)),
        name="rope_tables",
    )(positions.reshape(m, 1), inv)


PROJ_ROWS = 512
S5_TIME_BLOCK = CHUNK

C_RV, C_SU, C_MX, C_MI, C_AQ = 512, 1536, 2560, 4608, 4616


def _proj_kernel(x_ref, w_ref, wqk_ref, wxa_ref, wif_ref, o_ml, o_s5, o_xa, o_ret, o_if):
    xb = x_ref[...].astype(BF16)
    o_ml[...] = _dot(xb, w_ref[:, C_MX:C_MX + W_ML])
    tb = S5_TIME_BLOCK
    for j in range(W_S5 // 256):
        r = _dot(xb, w_ref[:, C_SU + 256 * j:C_SU + 256 * (j + 1)])
        for half in range(2):
            for t in range(o_s5.shape[1]):
                o_s5[2 * j + half, t] = r[tb * t:tb * (t + 1), 128 * half:128 * (half + 1)]
    o_xa[...] = _dot(xb, wxa_ref[...])
    o_ret[:, 0:2 * HEADS * RET_DK] = _dot(xb, wqk_ref[...])
    o_ret[:, 2 * HEADS * RET_DK:] = _dot(xb, w_ref[:, C_RV:C_SU])
    o_if[...] = _dot(xb, wif_ref[...])


def _project(x2, w, layer, bsz, seq):
    w_main, w_qk, w_xa, w_if = w
    m = x2.shape[0]
    tm = min(PROJ_ROWS, seq)
    nj = seq // tm
    tpb = tm // S5_TIME_BLOCK
    row = lambda i: (i, 0)
    s5_shape = (W_S5 // 128, seq // S5_TIME_BLOCK, bsz * S5_TIME_BLOCK, 128)
    specs = [pl.BlockSpec((tm, W_ML), row),
             pl.BlockSpec((W_S5 // 128, tpb, S5_TIME_BLOCK, 128),
                          lambda i: (0, i % nj, i // nj, 0)),
             pl.BlockSpec((tm, W_XA), row),
             pl.BlockSpec((tm, W_RET), row),
             pl.BlockSpec((tm, W_IF), row)]
    shapes = [(m, W_ML), s5_shape, (m, W_XA), (m, W_RET), (m, W_IF)]
    return pl.pallas_call(
        _proj_kernel,
        grid=(m // tm,),
        in_specs=[pl.BlockSpec((tm, D_MODEL), row)] +
                 [_layer_spec(t, layer, pipeline_mode=pl.Buffered(1))
                  for t in (w_main, w_qk, w_xa, w_if)],
        out_specs=specs,
        out_shape=[jax.ShapeDtypeStruct(sh, F32) for sh in shapes],
        compiler_params=_params(("parallel",)),
        name="in_proj",
    )(x2, w_main, w_qk, w_xa, w_if)


def _kv_kernel(m_ref, w_ref, k_ref, v_ref):
    mb = m_ref[...].astype(BF16)
    k_ref[...] = _dot(mb, w_ref[:, :GROUP_W]).astype(BF16)
    v_ref[...] = _dot(mb, w_ref[:, GROUP_W:]).astype(BF16)


def _memory_kv(mem2, w_kv, layer):
    m = mem2.shape[0]
    return pl.pallas_call(
        _kv_kernel,
        grid=(m // N_MEM,),
        in_specs=[pl.BlockSpec((N_MEM, D_MODEL), lambda i: (i, 0)),
                  _layer_spec(w_kv, layer)],
        out_specs=[pl.BlockSpec((N_MEM, GROUP_W), lambda i: (i, 0))] * 2,
        out_shape=[jax.ShapeDtypeStruct((m, GROUP_W), BF16)] * 2,
        compiler_params=_params(("parallel",)),
        name="memory_kv",
    )(mem2, w_kv)


RET_BLOCK_CHUNKS = 4


def _ret_kernel(p_ref, cos_ref, sin_ref, hmask_ref, intra_ref, qdec_ref, kdec_ref, cdec_ref,
                o_ref, s_ref):
    @pl.when(pl.program_id(1) == 0)
    def _():
        s_ref[...] = jnp.zeros_like(s_ref)

    cs = cos_ref[...]
    sn = sin_ref[...]

    def rot(t):
        t1 = t[:, :128]
        t2 = t[:, 128:]
        return jnp.concatenate([t1 * cs - t2 * sn, t1 * sn + t2 * cs], axis=1)

    qr = rot(p_ref[:, 0:256])
    kr = rot(p_ref[:, 256:512])
    for c in range(p_ref.shape[0] // CHUNK):
        rows = slice(CHUNK * c, CHUNK * (c + 1))
        vb = p_ref[rows, 512:1024].astype(BF16)
        krb = kr[rows].astype(BF16)
        state = s_ref[...]
        sb = state.astype(BF16)
        for h in range(HEADS):
            cols = slice(RET_DV * h, RET_DV * (h + 1))
            qm = (qr[rows] * hmask_ref[h:h + 1, :]).astype(BF16)
            att = (_dot_nt(qm, krb) * intra_ref[h]).astype(BF16)
            o = _dot(att, vb[:, cols]) + qdec_ref[:, cols] * _dot(qm, sb[:, cols])
            z = p_ref[rows, 1024 + RET_DV * h:1024 + RET_DV * (h + 1)]
            o_ref[rows, cols] = _norm_rows(o, GN_EPS) * _silu(z)
        kd = (kr[rows] * kdec_ref[...]).astype(BF16)
        s_ref[...] = cdec_ref[...] * state + _dot_tn(kd, vb)


def _retention(p_ret, cos, sin, tabs, bsz, seq):
    t_blk = min(RET_BLOCK_CHUNKS * CHUNK, seq)
    nb = seq // t_blk
    row = lambda b, c: (b * nb + c, 0)
    const2 = lambda b, c: (0, 0)
    hmask, intra, qdec, kdec, cdec = tabs
    return pl.pallas_call(
        _ret_kernel,
        grid=(bsz, nb),
        in_specs=[pl.BlockSpec((t_blk, W_RET), row),
                  pl.BlockSpec((t_blk, 128), row),
                  pl.BlockSpec((t_blk, 128), row),
                  pl.BlockSpec(hmask.shape, const2),
                  pl.BlockSpec(intra.shape, lambda b, c: (0, 0, 0)),
                  pl.BlockSpec(qdec.shape, const2),
                  pl.BlockSpec(kdec.shape, const2),
                  pl.BlockSpec(cdec.shape, const2)],
        out_specs=pl.BlockSpec((t_blk, GROUP_W), row),
        out_shape=jax.ShapeDtypeStruct((bsz * seq, GROUP_W), F32),
        scratch_shapes=[pltpu.VMEM((HEADS * RET_DK, HEADS * RET_DV), F32)],
        compiler_params=_params(("parallel", "arbitrary")),
        name="retention",
    )(p_ret, cos, sin, hmask, intra, qdec, kdec, cdec)


def _retention_tables():
    log_g = jnp.log1p(-jnp.exp2(-5.0 - jnp.arange(HEADS, dtype=F32)))
    r = jnp.arange(CHUNK, dtype=F32)
    intra = jnp.exp(jnp.abs(r[:, None] - r[None, :]) * log_g[:, None, None])
    q_dec = jnp.exp((r + 1.0) * log_g[:, None])
    k_dec = jnp.exp((CHUNK - 1.0 - r) * log_g[:, None])
    c_dec = jnp.exp(CHUNK * log_g)
    lane_head = (jnp.arange(HEADS * RET_DK) % 128) // (RET_DK // 2)
    hmask = (lane_head[None, :] == jnp.arange(HEADS)[:, None]).astype(F32)
    qdec = jnp.repeat(q_dec.T, RET_DV, axis=1)
    kdec = k_dec.T[:, lane_head]
    cdec = jnp.repeat(c_dec, RET_DV)[None, :]
    return hmask, intra, qdec, kdec, cdec


def _gelu_tanh(x):
    return 0.5 * x * (1.0 + jnp.tanh(math.sqrt(2.0 / math.pi) * (x + 0.044715 * (x * x * x))))


S5_LANE_BLOCK = 512
S5_KTILE_GROUPS = 16
S5_KTILE_IN = S5_KTILE_GROUPS * S5_CH
S5_KTILE_LANES = S5_KTILE_GROUPS * S5_STATE
S5_KTILES = S5_GROUPS // S5_KTILE_GROUPS


def _s5_kernel(p_ref, bbd_ref, cbd_ref, a_ref, d_ref, wglu_ref, o_ref, bu_ref, xs_ref, carry_ref):
    @pl.when(pl.program_id(0) == 0)
    def _():
        carry_ref[...] = jnp.zeros_like(carry_ref)

    tb = S5_TIME_BLOCK

    def gather(j):
        return jnp.concatenate([p_ref[j, 0, pl.ds(s, SUBLANES, stride=tb), :] for s in range(tb)],
                               axis=0)

    nblk = GROUP_W // 128
    u = jnp.concatenate([gather(j) for j in range(nblk)], axis=1)
    z = jnp.concatenate([gather(nblk + j) for j in range(nblk)], axis=1)
    ub = u.astype(BF16)
    kw = 2 * S5_KTILE_LANES
    for kt in range(S5_KTILES):
        bu_ref[:, kt * kw:(kt + 1) * kw] = _dot(ub[:, kt * S5_KTILE_IN:(kt + 1) * S5_KTILE_IN],
                                                bbd_ref[kt])

    lb = S5_LANE_BLOCK
    for kt in range(S5_KTILES):
        for blk in range(S5_KTILE_LANES // lb):
            re = slice(kt * kw + blk * lb, kt * kw + (blk + 1) * lb)
            im = slice(kt * kw + S5_KTILE_LANES + blk * lb, kt * kw + S5_KTILE_LANES + (blk + 1) * lb)
            par = slice(kt * S5_KTILE_LANES + blk * lb, kt * S5_KTILE_LANES + (blk + 1) * lb)
            ar = jnp.broadcast_to(a_ref[0:1, par], (SUBLANES, lb))
            ai = jnp.broadcast_to(a_ref[1:2, par], (SUBLANES, lb))

            def two_steps(i, carry):
                xr, xi = carry
                rows = pl.ds(pl.multiple_of(i * 2 * SUBLANES, 2 * SUBLANES), 2 * SUBLANES)
                b_r = bu_ref[rows, re]
                b_i = bu_ref[rows, im]
                x1r = ar * xr - ai * xi + b_r[0:SUBLANES]
                x1i = ar * xi + ai * xr + b_i[0:SUBLANES]
                x2r = ar * x1r - ai * x1i + b_r[SUBLANES:]
                x2i = ar * x1i + ai * x1r + b_i[SUBLANES:]
                xs_ref[rows, re] = jnp.concatenate([x1r, x2r], axis=0).astype(BF16)
                xs_ref[rows, im] = jnp.concatenate([x1i, x2i], axis=0).astype(BF16)
                return x2r, x2i

            xr, xi = lax.fori_loop(0, tb // 2, two_steps,
                                   (carry_ref[0, :, par], carry_ref[1, :, par]), unroll=4)
            carry_ref[0, :, par] = xr
            carry_ref[1, :, par] = xi

    y = jnp.concatenate([_dot(xs_ref[:, kt * kw:(kt + 1) * kw], cbd_ref[kt])
                         for kt in range(S5_KTILES)], axis=1) + d_ref[...] * u
    y = _gelu_tanh(y)
    y = y * _sigmoid(_dot(y.astype(BF16), wglu_ref[...]))
    res = y * _silu(z)
    for j in range(nblk):
        for s in range(tb):
            o_ref[j, 0, pl.ds(s, SUBLANES, stride=tb), :] = (
                res[s * SUBLANES:(s + 1) * SUBLANES, 128 * j:128 * (j + 1)])


def _s5(p_s5, w, layer, bsz, seq):
    assert bsz == SUBLANES, "the S5 kernel puts one batch element on each sublane"
    nb = seq // S5_TIME_BLOCK
    rows = bsz * S5_TIME_BLOCK
    blk = lambda j: (0, j, 0, 0)
    return pl.pallas_call(
        _s5_kernel,
        grid=(nb,),
        in_specs=[pl.BlockSpec((W_S5 // 128, 1, rows, 128), blk)] +
                 [_layer_spec(t, layer) for t in w],
        out_specs=pl.BlockSpec((GROUP_W // 128, 1, rows, 128), blk),
        out_shape=jax.ShapeDtypeStruct((GROUP_W // 128, nb, rows, 128), F32),
        scratch_shapes=[pltpu.VMEM((rows, 2 * S5_LANES), F32),
                        pltpu.VMEM((rows, 2 * S5_LANES), BF16),
                        pltpu.VMEM((2, SUBLANES, S5_LANES), F32)],
        compiler_params=_params(("arbitrary",)),
        name="s5",
    )(p_s5, *w)


def _s5_weights(a_re, a_im, log_dt, b_re, b_im, c_re, c_im, d, w_glu):
    dt = jnp.exp(log_dt)[:, None]
    mag = jnp.exp(a_re * dt)
    ab_re = mag * jnp.cos(a_im * dt)
    ab_im = mag * jnp.sin(a_im * dt)
    den = a_re * a_re + a_im * a_im
    p = ab_re - 1.0
    f_re = (p * a_re + ab_im * a_im) / den
    f_im = (ab_im * a_re - p * a_im) / den
    bb_re = f_re[..., None] * b_re - f_im[..., None] * b_im
    bb_im = f_re[..., None] * b_im + f_im[..., None] * b_re
    kg = S5_KTILE_GROUPS
    own_group = (jnp.arange(kg)[:, None, None, None] == jnp.arange(kg)[None, None, :, None])

    def tiles_in(t):
        t = jnp.swapaxes(t, 1, 2).reshape(S5_KTILES, kg, S5_CH, 1, S5_STATE)
        return jnp.where(own_group, t, 0.0).reshape(S5_KTILES, S5_KTILE_IN, S5_KTILE_LANES)

    def tiles_out(t):
        t = jnp.swapaxes(t, 1, 2).reshape(S5_KTILES, kg, S5_STATE, 1, S5_CH)
        return jnp.where(own_group, t, 0.0).reshape(S5_KTILES, S5_KTILE_LANES, S5_KTILE_IN)

    bbd = jnp.concatenate([tiles_in(bb_re), tiles_in(bb_im)], axis=2).astype(BF16)
    cbd = jnp.concatenate([tiles_out(c_re), tiles_out(-c_im)], axis=1).astype(BF16)
    a = jnp.stack([ab_re.reshape(-1), ab_im.reshape(-1)])
    return bbd, cbd, a, d[None, :], w_glu.astype(BF16)


def _log_sigmoid(x):
    return jnp.minimum(x, 0.0) - jnp.log1p(jnp.exp(-jnp.abs(x)))


ML_BLOCK_CHUNKS = 4


def _segmented_scan(x, row_in_chunk, combine, identity):
    sh = 1
    while sh < CHUNK:
        x = combine(x, jnp.where(row_in_chunk >= sh, pltpu.roll(x, sh, 0), identity))
        sh *= 2
    return x


def _time_on_lanes(t):
    return jnp.concatenate([t, jnp.zeros_like(t)], axis=0).T


def _ml_kernel(p_ref, g_ref, conv_ref, wq_ref, wkt_ref, bias_ref, gn_ref, o_ref,
               xbuf_ref, q_ref, kt_ref, rep_ref, cn_ref, m_ref):
    t_blk = p_ref.shape[0]
    n_chunks = t_blk // CHUNK

    @pl.when(pl.program_id(1) == 0)
    def _():
        xbuf_ref[0:SUBLANES, :] = jnp.zeros((SUBLANES, GROUP_W), F32)
        cn_ref[...] = jnp.zeros_like(cn_ref)
        m_ref[...] = jnp.zeros_like(m_ref)

    xbuf_ref[SUBLANES:, :] = p_ref[:, 0:GROUP_W]
    conv = jnp.zeros((t_blk, GROUP_W), F32)
    for j in range(ML_CONV):
        off = SUBLANES - (ML_CONV - 1) + j
        conv = conv + conv_ref[j:j + 1, :] * xbuf_ref[off:off + t_blk, :]
    xbuf_ref[0:SUBLANES, :] = xbuf_ref[t_blk:t_blk + SUBLANES, :]
    xc = _silu(conv).astype(BF16)
    for h in range(HEADS):
        cols = slice(ML_DH * h, ML_DH * (h + 1))
        q_ref[:, cols] = _dot(xc[:, cols], wq_ref[h]).astype(BF16)
        kt_ref[h] = _dot_nt(wkt_ref[h], xc[:, cols]) * ML_DH ** -0.5

    gates = g_ref[...] + bias_ref[...]
    ig = gates
    lf = _log_sigmoid(pltpu.roll(gates, 128 - HEADS, 1))
    row_in_chunk = lax.broadcasted_iota(jnp.int32, (t_blk, 128), 0) % CHUNK
    bcum = _segmented_scan(lf, row_in_chunk, jnp.add, 0.0)
    dgate = ig - bcum
    cmax = _segmented_scan(dgate, row_in_chunk, jnp.maximum, -jnp.inf)
    m_loc = bcum + cmax
    m_prev = m_ref[0:1, :]
    w_intra, w_inter, floor, wk_t, dec_scale = [], [], [], [], []
    for c in range(n_chunks):
        rows = slice(CHUNK * c, CHUNK * (c + 1))
        last = slice(CHUNK * (c + 1) - 1, CHUNK * (c + 1))
        b_last = bcum[last]
        g_max = b_last + cmax[last]
        m_new = jnp.maximum(b_last + m_prev, g_max)
        dec_scale.append(jnp.exp(b_last + m_prev - m_new))
        dec_scale.append(jnp.exp(g_max - m_new))
        inter = bcum[rows] + m_prev
        m_r = jnp.maximum(m_loc[rows], inter)
        w_inter.append(jnp.exp(inter - m_r))
        w_intra.append(jnp.exp(m_loc[rows] - m_r))
        floor.append(jnp.exp(-m_r))
        wk_t.append(jnp.exp(b_last - bcum[rows] + ig[rows] - g_max))
        m_prev = m_new
    m_ref[0:1, :] = m_prev
    pad_rows = -2 * n_chunks % SUBLANES
    if pad_rows:
        dec_scale.append(jnp.zeros((pad_rows, 128), F32))
    per_time = [cmax, jnp.concatenate(w_intra, axis=0), jnp.concatenate(w_inter, axis=0),
                jnp.concatenate(floor, axis=0), jnp.concatenate(dec_scale, axis=0)]
    offs = [0]
    for t in per_time:
        offs.append(offs[-1] + t.shape[0])
    for h in range(HEADS):
        for t, o in zip(per_time, offs):
            rep_ref[h, o:o + t.shape[0], :] = jnp.broadcast_to(t[:, h:h + 1], t.shape)

    r_i = lax.broadcasted_iota(jnp.int32, (CHUNK, CHUNK), 0)
    c_i = lax.broadcasted_iota(jnp.int32, (CHUNK, CHUNK), 1)
    causal = c_i <= r_i
    ones = jnp.ones((CHUNK, ML_DH), BF16)
    for c in range(n_chunks):
        rows = slice(CHUNK * c, CHUNK * (c + 1))
        d_t = _time_on_lanes(dgate[rows])
        wk_tt = _time_on_lanes(wk_t[c])
        for h in range(HEADS):
            cols = slice(ML_DH * h, ML_DH * (h + 1))
            rep = lambda i: rep_ref[h, offs[i] + CHUNK * c:offs[i] + CHUNK * (c + 1), :]
            cmax_b, a_b, i_b, floor_b = rep(0), rep(1), rep(2), rep(3)
            dec_b = rep_ref[h, offs[4] + 2 * c:offs[4] + 2 * c + 1, :]
            scale_b = rep_ref[h, offs[4] + 2 * c + 1:offs[4] + 2 * c + 2, :]
            qb = q_ref[rows, cols]
            kt = kt_ref[h, :, rows]
            v_ext = jnp.concatenate(
                [p_ref[rows, GROUP_W + ML_DH * h:GROUP_W + ML_DH * (h + 1)].astype(BF16), ones], axis=1)
            cn_prev = cn_ref[h]

            w = jnp.exp(jnp.where(causal, d_t[h:h + 1, 0:CHUNK] - cmax_b[:, 0:CHUNK], -jnp.inf))
            s = _dot(qb, kt.astype(BF16)) * w
            intra = _dot(s.astype(BF16), v_ext)
            inter = _dot(qb, cn_prev.astype(BF16))
            both = jnp.concatenate([a_b, a_b], axis=1) * intra + jnp.concatenate([i_b, i_b], axis=1) * inter
            hh = both[:, 0:ML_DH] / jnp.maximum(jnp.abs(both[:, ML_DH:]), floor_b)

            ktw = (kt * wk_tt[h:h + 1, 0:CHUNK]).astype(BF16)
            cn_ref[h] = (jnp.concatenate([dec_b, dec_b], axis=1) * cn_prev
                         + jnp.concatenate([scale_b, scale_b], axis=1) * _dot(ktw, v_ext))

            og = p_ref[rows, 2 * GROUP_W + ML_DH * h:2 * GROUP_W + ML_DH * (h + 1)]
            z = p_ref[rows, 3 * GROUP_W + ML_DH * h:3 * GROUP_W + ML_DH * (h + 1)]
            o_ref[rows, cols] = (_norm_rows(hh * _sigmoid(og), GN_EPS) * gn_ref[:, cols] * _silu(z))


def _mlstm(p_ml, p_if, w, layer, bsz, seq):
    conv_w, wq, wkt, bias, gn = w
    t_blk = min(ML_BLOCK_CHUNKS * CHUNK, seq)
    nb = seq // t_blk
    n_chunks = t_blk // CHUNK
    rep_rows = 4 * t_blk + 2 * n_chunks + (-2 * n_chunks % SUBLANES)
    row = lambda b, c: (b * nb + c, 0)
    return pl.pallas_call(
        _ml_kernel,
        grid=(bsz, nb),
        in_specs=[pl.BlockSpec((t_blk, W_ML), row),
                  pl.BlockSpec((t_blk, W_IF), row),
                  _layer_spec(conv_w, layer),
                  _layer_spec(wq, layer),
                  _layer_spec(wkt, layer),
                  _layer_spec(bias, layer),
                  _layer_spec(gn, layer)],
        out_specs=pl.BlockSpec((t_blk, GROUP_W), row),
        out_shape=jax.ShapeDtypeStruct((bsz * seq, GROUP_W), F32),
        scratch_shapes=[pltpu.VMEM((t_blk + SUBLANES, GROUP_W), F32),
                        pltpu.VMEM((t_blk, GROUP_W), BF16),
                        pltpu.VMEM((HEADS, ML_DH, t_blk), F32),
                        pltpu.VMEM((HEADS, rep_rows, 128), F32),
                        pltpu.VMEM((HEADS, ML_DH, 2 * ML_DH), F32),
                        pltpu.VMEM((SUBLANES, 128), F32)],
        compiler_params=_params(("parallel", "arbitrary")),
        name="mlstm",
    )(p_ml, p_if, conv_w, wq, wkt, bias, gn)


def _xa_kernel(p_ref, mk_ref, mv_ref, o_ref):
    z = p_ref[:, GROUP_W:]
    outs = []
    for h in range(HEADS):
        cols = slice(XA_DH * h, XA_DH * (h + 1))
        s = _dot_nt(p_ref[:, cols].astype(BF16), mk_ref[:, cols]) * XA_DH ** -0.5
        e = jnp.exp(s - jnp.max(s, axis=1, keepdims=True))
        outs.append(_dot(e.astype(BF16), mv_ref[:, cols]) / jnp.sum(e, axis=1, keepdims=True))
    o_ref[...] = jnp.concatenate(outs, axis=1) * _silu(z)


def _cross_attention(p_xa, mk, mv, bsz, seq):
    t_blk = min(256, seq)
    nb = seq // t_blk
    row = lambda b, j: (b * nb + j, 0)
    mem = lambda b, j: (b, 0)
    return pl.pallas_call(
        _xa_kernel,
        grid=(bsz, nb),
        in_specs=[pl.BlockSpec((t_blk, W_XA), row),
                  pl.BlockSpec((N_MEM, GROUP_W), mem),
                  pl.BlockSpec((N_MEM, GROUP_W), mem)],
        out_specs=pl.BlockSpec((t_blk, GROUP_W), row),
        out_shape=jax.ShapeDtypeStruct((bsz * seq, GROUP_W), F32),
        compiler_params=_params(("parallel", "parallel")),
        name="memory_attention",
    )(p_xa, mk, mv)


def _out_kernel(x_ref, r_ref, s_ref, m_ref, a_ref, w_ref, g_ref, b_ref, o_ref):
    y = None
    s5o = jnp.concatenate(
        [jnp.concatenate([s_ref[j, t] for t in range(s_ref.shape[1])], axis=0)
         for j in range(s_ref.shape[0])], axis=1)
    for i, t in enumerate((r_ref[...], s5o, m_ref[...], a_ref[...])):
        part = _dot(t.astype(BF16), w_ref[GROUP_W * i:GROUP_W * (i + 1), :])
        y = part if y is None else y + part
    t = ALPHA * x_ref[...] + y
    o_ref[...] = _norm_rows(t, LN_EPS) * g_ref[...] + b_ref[...]


def _out_project(x2, groups, w_out, ln_g, ln_b, layer, seq):
    m = x2.shape[0]
    tm = min(PROJ_ROWS, seq)
    nj = seq // tm
    row = lambda i: (i, 0)
    return pl.pallas_call(
        _out_kernel,
        grid=(m // tm,),
        in_specs=[pl.BlockSpec((tm, D_MODEL), row),
                  pl.BlockSpec((tm, GROUP_W), row),
                  pl.BlockSpec((GROUP_W // 128, tm // S5_TIME_BLOCK, S5_TIME_BLOCK, 128),
                               lambda i: (0, i % nj, i // nj, 0)),
                  pl.BlockSpec((tm, GROUP_W), row),
                  pl.BlockSpec((tm, GROUP_W), row),
                  _layer_spec(w_out, layer),
                  _layer_spec(ln_g, layer),
                  _layer_spec(ln_b, layer)],
        out_specs=pl.BlockSpec((tm, D_MODEL), row),
        out_shape=jax.ShapeDtypeStruct((m, D_MODEL), F32),
        compiler_params=_params(("parallel",)),
        name="out_proj_norm",
    )(x2, *groups, w_out, ln_g, ln_b)


def _split_w_in(w_in):
    half = RET_DK // 2
    perm = jnp.array([h * RET_DK + s * half + j for s in range(2) for h in range(HEADS)
                      for j in range(half)], jnp.int32)
    n_qk = HEADS * RET_DK
    w_qk = jnp.concatenate([w_in[:, :n_qk][:, perm],
                            w_in[:, n_qk:2 * n_qk][:, perm] * RET_DK ** -0.5],
                           axis=1).astype(BF16)
    w_xa = w_in[:, C_AQ:C_AQ + W_XA].astype(BF16)
    w_if = jnp.pad(w_in[:, C_MI:C_AQ], ((0, 0), (0, W_IF - 2 * HEADS))).astype(BF16)
    return w_in.astype(BF16), w_qk, w_xa, w_if


def _prepare_weights(w_in, s5_a_re, s5_a_im, s5_log_dt, s5_b_re, s5_b_im, s5_c_re, s5_c_im,
                     s5_d, s5_w_glu, ml_conv_w, ml_wq, ml_wk, ml_b_ig, ml_b_fg, ml_gn, xa_w_kv, w_out,
                     ln_g, ln_b):
    bias = jnp.concatenate([ml_b_ig, ml_b_fg, jnp.zeros((W_IF - 2 * HEADS,), F32)])[None, :]
    return dict(
        w_in=_split_w_in(w_in),
        s5=_s5_weights(s5_a_re, s5_a_im, s5_log_dt, s5_b_re, s5_b_im, s5_c_re, s5_c_im, s5_d,
                       s5_w_glu),
        ml=(ml_conv_w, ml_wq.astype(BF16), jnp.swapaxes(ml_wk, 1, 2).astype(BF16), bias,
            ml_gn[None, :]),
        w_kv=xa_w_kv.astype(BF16), w_out=w_out.astype(BF16), ln_g=ln_g[None, :], ln_b=ln_b[None, :])


def _layer(x2, mem2, cos, sin, ret_tabs, bsz, seq, w, layer):
    p_ml, p_s5, p_xa, p_ret, p_if = _project(x2, w["w_in"], layer, bsz, seq)
    ret = _retention(p_ret, cos, sin, ret_tabs, bsz, seq)
    s5o = _s5(p_s5, w["s5"], layer, bsz, seq)
    ml = _mlstm(p_ml, p_if, w["ml"], layer, bsz, seq)
    mk, mv = _memory_kv(mem2, w["w_kv"], layer)
    xa = _cross_attention(p_xa, mk, mv, bsz, seq)
    return _out_project(x2, (ret, s5o, ml, xa), w["w_out"], w["ln_g"], w["ln_b"], layer, seq)


def kernel(x, mem, positions, w_in, s5_a_re, s5_a_im, s5_log_dt, s5_b_re, s5_b_im, s5_c_re, s5_c_im,
           s5_d, s5_w_glu, ml_conv_w, ml_wq, ml_wk, ml_b_ig, ml_b_fg, ml_gn, xa_w_kv, w_out, ln_g,
           ln_b):
    bsz, seq, _ = x.shape
    x2 = x.reshape(bsz * seq, D_MODEL)
    mem2 = mem.reshape(bsz * N_MEM, D_MODEL)
    cos, sin = _rope_tables(positions)
    ret_tabs = _retention_tables()
    weights = jax.vmap(_prepare_weights)(
        w_in, s5_a_re, s5_a_im, s5_log_dt, s5_b_re, s5_b_im, s5_c_re, s5_c_im, s5_d, s5_w_glu,
        ml_conv_w, ml_wq, ml_wk, ml_b_ig, ml_b_fg, ml_gn, xa_w_kv, w_out, ln_g, ln_b)
    for l in range(w_in.shape[0]):
        x2 = _layer(x2, mem2, cos, sin, ret_tabs, bsz, seq, weights, l)
    return x2.reshape(bsz, seq, D_MODEL)
```

```python
import functools
import math

import jax
import jax.numpy as jnp
from jax import lax
from jax.experimental import pallas as pl
from jax.experimental.pallas import tpu as pltpu

F32 = jnp.float32
BF16 = jnp.bfloat16

D_MODEL = 1024
CHUNK = 64
N_MEM = 256
GROUP_W = 512
HEADS = 4
RET_DK = 64
RET_DV = 128
ROPE_BASE = 10000.0
S5_CH = 16
S5_GROUPS = 32
S5_STATE = 64
S5_LANES = S5_GROUPS * S5_STATE
ML_DH = 128
ML_CONV = 4
XA_DH = 128
LN_EPS = 1e-5
GN_EPS = 1e-6
DEPTH = 4
ALPHA = (2 * DEPTH) ** 0.25

SUBLANES = 8
VMEM_LIMIT = 56 * 1024 * 1024

W_ML, W_S5, W_XA, W_RET, W_IF = 4 * GROUP_W, 2 * GROUP_W, 2 * GROUP_W, 3 * GROUP_W, 128


def _params(sem):
    return pltpu.CompilerParams(dimension_semantics=sem, vmem_limit_bytes=VMEM_LIMIT)


def _sigmoid(x):
    return 1.0 / (1.0 + jnp.exp(-x))


def _silu(x):
    return x * _sigmoid(x)


def _norm_rows(t, eps):
    mu = jnp.mean(t, axis=-1, keepdims=True)
    d = t - mu
    var = jnp.mean(d * d, axis=-1, keepdims=True)
    return d * lax.rsqrt(var + eps)


def _layer_spec(t, layer, **kw):
    return pl.BlockSpec((None,) + t.shape[1:], lambda *_: (layer,) + (0,) * (t.ndim - 1), **kw)


def _dot(a, b):
    return jnp.dot(a, b, preferred_element_type=F32)


def _dot_nt(a, b):
    return lax.dot_general(a, b, (((1,), (1,)), ((), ())), preferred_element_type=F32)


def _dot_tn(a, b):
    return lax.dot_general(a, b, (((0,), (0,)), ((), ())), preferred_element_type=F32)


def _rope_kernel(pos_ref, inv_ref, cos_ref, sin_ref):
    ang = pos_ref[...].astype(F32) * inv_ref[...]
    cos_ref[...] = jnp.cos(ang)
    sin_ref[...] = jnp.sin(ang)


def _rope_tables(positions):
    m = positions.size
    half = RET_DK // 2
    inv = ROPE_BASE ** (-jnp.arange(half, dtype=F32) / half)
    inv = jnp.tile(inv, HEADS)[None, :]
    tm = min(512, m)
    return pl.pallas_call(
        _rope_kernel,
        grid=(m // tm,),
        in_specs=[pl.BlockSpec((tm, 1), lambda i: (i, 0)),
                  pl.BlockSpec((1, 128), lambda i: (0, 0))],
        out_specs=[pl.BlockSpec((tm, 128), lambda i: (i, 0))] * 2,
        out_shape=[jax.ShapeDtypeStruct((m, 128), F32)] * 2,
        compiler_params=_params(("parallel",)),
        name="rope_tables",
    )(positions.reshape(m, 1), inv)


PROJ_ROWS = 512
S5_TIME_BLOCK = CHUNK

C_RV, C_SU, C_MX, C_MI, C_AQ = 512, 1536, 2560, 4608, 4616


def _proj_kernel(x_ref, w_ref, wqk_ref, wxa_ref, wif_ref, o_ml, o_s5, o_xa, o_ret, o_if):
    xb = x_ref[...].astype(BF16)
    o_ml[...] = _dot(xb, w_ref[:, C_MX:C_MX + W_ML])
    tb = S5_TIME_BLOCK
    for j in range(W_S5 // 256):
        r = _dot(xb, w_ref[:, C_SU + 256 * j:C_SU + 256 * (j + 1)])
        for half in range(2):
            for t in range(o_s5.shape[1]):
                o_s5[2 * j + half, t] = r[tb * t:tb * (t + 1), 128 * half:128 * (half + 1)]
    o_xa[...] = _dot(xb, wxa_ref[...])
    o_ret[:, 0:2 * HEADS * RET_DK] = _dot(xb, wqk_ref[...])
    o_ret[:, 2 * HEADS * RET_DK:] = _dot(xb, w_ref[:, C_RV:C_SU])
    o_if[...] = _dot(xb, wif_ref[...])


def _project(x2, w, layer, bsz, seq):
    w_main, w_qk, w_xa, w_if = w
    m = x2.shape[0]
    tm = min(PROJ_ROWS, seq)
    nj = seq // tm
    tpb = tm // S5_TIME_BLOCK
    row = lambda i: (i, 0)
    s5_shape = (W_S5 // 128, seq // S5_TIME_BLOCK, bsz * S5_TIME_BLOCK, 128)
    specs = [pl.BlockSpec((tm, W_ML), row),
             pl.BlockSpec((W_S5 // 128, tpb, S5_TIME_BLOCK, 128),
                          lambda i: (0, i % nj, i // nj, 0)),
             pl.BlockSpec((tm, W_XA), row),
             pl.BlockSpec((tm, W_RET), row),
             pl.BlockSpec((tm, W_IF), row)]
    shapes = [(m, W_ML), s5_shape, (m, W_XA), (m, W_RET), (m, W_IF)]
    return pl.pallas_call(
        _proj_kernel,
        grid=(m // tm,),
        in_specs=[pl.BlockSpec((tm, D_MODEL), row)] +
                 [_layer_spec(t, layer, pipeline_mode=pl.Buffered(1))
                  for t in (w_main, w_qk, w_xa, w_if)],
        out_specs=specs,
        out_shape=[jax.ShapeDtypeStruct(sh, F32) for sh in shapes],
        compiler_params=_params(("parallel",)),
        name="in_proj",
    )(x2, w_main, w_qk, w_xa, w_if)


def _kv_kernel(m_ref, w_ref, k_ref, v_ref):
    mb = m_ref[...].astype(BF16)
    k_ref[...] = _dot(mb, w_ref[:, :GROUP_W]).astype(BF16)
    v_ref[...] = _dot(mb, w_ref[:, GROUP_W:]).astype(BF16)


def _memory_kv(mem2, w_kv, layer):
    m = mem2.shape[0]
    return pl.pallas_call(
        _kv_kernel,
        grid=(m // N_MEM,),
        in_specs=[pl.BlockSpec((N_MEM, D_MODEL), lambda i: (i, 0)),
                  _layer_spec(w_kv, layer)],
        out_specs=[pl.BlockSpec((N_MEM, GROUP_W), lambda i: (i, 0))] * 2,
        out_shape=[jax.ShapeDtypeStruct((m, GROUP_W), BF16)] * 2,
        compiler_params=_params(("parallel",)),
        name="memory_kv",
    )(mem2, w_kv)


RET_BLOCK_CHUNKS = 4


def _ret_kernel(p_ref, cos_ref, sin_ref, hmask_ref, intra_ref, qdec_ref, kdec_ref, cdec_ref,
                o_ref, s_ref):
    @pl.when(pl.program_id(1) == 0)
    def _():
        s_ref[...] = jnp.zeros_like(s_ref)

    cs = cos_ref[...]
    sn = sin_ref[...]

    def rot(t):
        t1 = t[:, :128]
        t2 = t[:, 128:]
        return jnp.concatenate([t1 * cs - t2 * sn, t1 * sn + t2 * cs], axis=1)

    qr = rot(p_ref[:, 0:256])
    kr = rot(p_ref[:, 256:512])
    for c in range(p_ref.shape[0] // CHUNK):
        rows = slice(CHUNK * c, CHUNK * (c + 1))
        vb = p_ref[rows, 512:1024].astype(BF16)
        krb = kr[rows].astype(BF16)
        state = s_ref[...]
        sb = state.astype(BF16)
        for h in range(HEADS):
            cols = slice(RET_DV * h, RET_DV * (h + 1))
            qm = (qr[rows] * hmask_ref[h:h + 1, :]).astype(BF16)
            att = (_dot_nt(qm, krb) * intra_ref[h]).astype(BF16)
            o = _dot(att, vb[:, cols]) + qdec_ref[:, cols] * _dot(qm, sb[:, cols])
            z = p_ref[rows, 1024 + RET_DV * h:1024 + RET_DV * (h + 1)]
            o_ref[rows, cols] = _norm_rows(o, GN_EPS) * _silu(z)
        kd = (kr[rows] * kdec_ref[...]).astype(BF16)
        s_ref[...] = cdec_ref[...] * state + _dot_tn(kd, vb)


def _retention(p_ret, cos, sin, tabs, bsz, seq):
    t_blk = min(RET_BLOCK_CHUNKS * CHUNK, seq)
    nb = seq // t_blk
    row = lambda b, c: (b * nb + c, 0)
    const2 = lambda b, c: (0, 0)
    hmask, intra, qdec, kdec, cdec = tabs
    return pl.pallas_call(
        _ret_kernel,
        grid=(bsz, nb),
        in_specs=[pl.BlockSpec((t_blk, W_RET), row),
                  pl.BlockSpec((t_blk, 128), row),
                  pl.BlockSpec((t_blk, 128), row),
                  pl.BlockSpec(hmask.shape, const2),
                  pl.BlockSpec(intra.shape, lambda b, c: (0, 0, 0)),
                  pl.BlockSpec(qdec.shape, const2),
                  pl.BlockSpec(kdec.shape, const2),
                  pl.BlockSpec(cdec.shape, const2)],
        out_specs=pl.BlockSpec((t_blk, GROUP_W), row),
        out_shape=jax.ShapeDtypeStruct((bsz * seq, GROUP_W), F32),
        scratch_shapes=[pltpu.VMEM((HEADS * RET_DK, HEADS * RET_DV), F32)],
        compiler_params=_params(("parallel", "arbitrary")),
        name="retention",
    )(p_ret, cos, sin, hmask, intra, qdec, kdec, cdec)


def _retention_tables():
    log_g = jnp.log1p(-jnp.exp2(-5.0 - jnp.arange(HEADS, dtype=F32)))
    r = jnp.arange(CHUNK, dtype=F32)
    intra = jnp.exp(jnp.abs(r[:, None] - r[None, :]) * log_g[:, None, None])
    q_dec = jnp.exp((r + 1.0) * log_g[:, None])
    k_dec = jnp.exp((CHUNK - 1.0 - r) * log_g[:, None])
    c_dec = jnp.exp(CHUNK * log_g)
    lane_head = (jnp.arange(HEADS * RET_DK) % 128) // (RET_DK // 2)
    hmask = (lane_head[None, :] == jnp.arange(HEADS)[:, None]).astype(F32)
    qdec = jnp.repeat(q_dec.T, RET_DV, axis=1)
    kdec = k_dec.T[:, lane_head]
    cdec = jnp.repeat(c_dec, RET_DV)[None, :]
    return hmask, intra, qdec, kdec, cdec


def _gelu_tanh(x):
    return 0.5 * x * (1.0 + jnp.tanh(math.sqrt(2.0 / math.pi) * (x + 0.044715 * (x * x * x))))


S5_LANE_BLOCK = 512
S5_KTILE_GROUPS = 16
S5_KTILE_IN = S5_KTILE_GROUPS * S5_CH
S5_KTILE_LANES = S5_KTILE_GROUPS * S5_STATE
S5_KTILES = S5_GROUPS // S5_KTILE_GROUPS


def _s5_kernel(p_ref, bbd_ref, cbd_ref, a_ref, d_ref, wglu_ref, o_ref, bu_ref, xs_ref, carry_ref):
    @pl.when(pl.program_id(0) == 0)
    def _():
        carry_ref[...] = jnp.zeros_like(carry_ref)

    tb = S5_TIME_BLOCK

    def gather(j):
        return jnp.concatenate([p_ref[j, 0, pl.ds(s, SUBLANES, stride=tb), :] for s in range(tb)],
                               axis=0)

    nblk = GROUP_W // 128
    u = jnp.concatenate([gather(j) for j in range(nblk)], axis=1)
    z = jnp.concatenate([gather(nblk + j) for j in range(nblk)], axis=1)
    ub = u.astype(BF16)
    kw = 2 * S5_KTILE_LANES
    for kt in range(S5_KTILES):
        bu_ref[:, kt * kw:(kt + 1) * kw] = _dot(ub[:, kt * S5_KTILE_IN:(kt + 1) * S5_KTILE_IN],
                                                bbd_ref[kt])

    lb = S5_LANE_BLOCK
    for kt in range(S5_KTILES):
        for blk in range(S5_KTILE_LANES // lb):
            re = slice(kt * kw + blk * lb, kt * kw + (blk + 1) * lb)
            im = slice(kt * kw + S5_KTILE_LANES + blk * lb, kt * kw + S5_KTILE_LANES + (blk + 1) * lb)
            par = slice(kt * S5_KTILE_LANES + blk * lb, kt * S5_KTILE_LANES + (blk + 1) * lb)
            ar = jnp.broadcast_to(a_ref[0:1, par], (SUBLANES, lb))
            ai = jnp.broadcast_to(a_ref[1:2, par], (SUBLANES, lb))

            def two_steps(i, carry):
                xr, xi = carry
                rows = pl.ds(pl.multiple_of(i * 2 * SUBLANES, 2 * SUBLANES), 2 * SUBLANES)
                b_r = bu_ref[rows, re]
                b_i = bu_ref[rows, im]
                x1r = ar * xr - ai * xi + b_r[0:SUBLANES]
                x1i = ar * xi + ai * xr + b_i[0:SUBLANES]
                x2r = ar * x1r - ai * x1i + b_r[SUBLANES:]
                x2i = ar * x1i + ai * x1r + b_i[SUBLANES:]
                xs_ref[rows, re] = jnp.concatenate([x1r, x2r], axis=0).astype(BF16)
                xs_ref[rows, im] = jnp.concatenate([x1i, x2i], axis=0).astype(BF16)
                return x2r, x2i

            xr, xi = lax.fori_loop(0, tb // 2, two_steps,
                                   (carry_ref[0, :, par], carry_ref[1, :, par]), unroll=4)
            carry_ref[0, :, par] = xr
            carry_ref[1, :, par] = xi

    y = jnp.concatenate([_dot(xs_ref[:, kt * kw:(kt + 1) * kw], cbd_ref[kt])
                         for kt in range(S5_KTILES)], axis=1) + d_ref[...] * u
    y = _gelu_tanh(y)
    y = y * _sigmoid(_dot(y.astype(BF16), wglu_ref[...]))
    res = y * _silu(z)
    for j in range(nblk):
        for s in range(tb):
            o_ref[j, 0, pl.ds(s, SUBLANES, stride=tb), :] = (
                res[s * SUBLANES:(s + 1) * SUBLANES, 128 * j:128 * (j + 1)])


def _s5(p_s5, w, layer, bsz, seq):
    assert bsz == SUBLANES, "the S5 kernel puts one batch element on each sublane"
    nb = seq // S5_TIME_BLOCK
    rows = bsz * S5_TIME_BLOCK
    blk = lambda j: (0, j, 0, 0)
    return pl.pallas_call(
        _s5_kernel,
        grid=(nb,),
        in_specs=[pl.BlockSpec((W_S5 // 128, 1, rows, 128), blk)] +
                 [_layer_spec(t, layer) for t in w],
        out_specs=pl.BlockSpec((GROUP_W // 128, 1, rows, 128), blk),
        out_shape=jax.ShapeDtypeStruct((GROUP_W // 128, nb, rows, 128), F32),
        scratch_shapes=[pltpu.VMEM((rows, 2 * S5_LANES), F32),
                        pltpu.VMEM((rows, 2 * S5_LANES), BF16),
                        pltpu.VMEM((2, SUBLANES, S5_LANES), F32)],
        compiler_params=_params(("arbitrary",)),
        name="s5",
    )(p_s5, *w)


def _s5_weights(a_re, a_im, log_dt, b_re, b_im, c_re, c_im, d, w_glu):
    dt = jnp.exp(log_dt)[:, None]
    mag = jnp.exp(a_re * dt)
    ab_re = mag * jnp.cos(a_im * dt)
    ab_im = mag * jnp.sin(a_im * dt)
    den = a_re * a_re + a_im * a_im
    p = ab_re - 1.0
    f_re = (p * a_re + ab_im * a_im) / den
    f_im = (ab_im * a_re - p * a_im) / den
    bb_re = f_re[..., None] * b_re - f_im[..., None] * b_im
    bb_im = f_re[..., None] * b_im + f_im[..., None] * b_re
    kg = S5_KTILE_GROUPS
    own_group = (jnp.arange(kg)[:, None, None, None] == jnp.arange(kg)[None, None, :, None])

    def tiles_in(t):
        t = jnp.swapaxes(t, 1, 2).reshape(S5_KTILES, kg, S5_CH, 1, S5_STATE)
        return jnp.where(own_group, t, 0.0).reshape(S5_KTILES, S5_KTILE_IN, S5_KTILE_LANES)

    def tiles_out(t):
        t = jnp.swapaxes(t, 1, 2).reshape(S5_KTILES, kg, S5_STATE, 1, S5_CH)
        return jnp.where(own_group, t, 0.0).reshape(S5_KTILES, S5_KTILE_LANES, S5_KTILE_IN)

    bbd = jnp.concatenate([tiles_in(bb_re), tiles_in(bb_im)], axis=2).astype(BF16)
    cbd = jnp.concatenate([tiles_out(c_re), tiles_out(-c_im)], axis=1).astype(BF16)
    a = jnp.stack([ab_re.reshape(-1), ab_im.reshape(-1)])
    return bbd, cbd, a, d[None, :], w_glu.astype(BF16)


def _log_sigmoid(x):
    return jnp.minimum(x, 0.0) - jnp.log1p(jnp.exp(-jnp.abs(x)))


ML_BLOCK_CHUNKS = 4


def _segmented_scan(x, row_in_chunk, combine, identity):
    sh = 1
    while sh < CHUNK:
        x = combine(x, jnp.where(row_in_chunk >= sh, pltpu.roll(x, sh, 0), identity))
        sh *= 2
    return x


def _time_on_lanes(t):
    return jnp.concatenate([t, jnp.zeros_like(t)], axis=0).T


def _ml_kernel(p_ref, g_ref, conv_ref, wq_ref, wkt_ref, bias_ref, gn_ref, o_ref,
               xbuf_ref, q_ref, kt_ref, rep_ref, cn_ref, m_ref):
    t_blk = p_ref.shape[0]
    n_chunks = t_blk // CHUNK

    @pl.when(pl.program_id(1) == 0)
    def _():
        xbuf_ref[0:SUBLANES, :] = jnp.zeros((SUBLANES, GROUP_W), F32)
        cn_ref[...] = jnp.zeros_like(cn_ref)
        m_ref[...] = jnp.zeros_like(m_ref)

    xbuf_ref[SUBLANES:, :] = p_ref[:, 0:GROUP_W]
    conv = jnp.zeros((t_blk, GROUP_W), F32)
    for j in range(ML_CONV):
        off = SUBLANES - (ML_CONV - 1) + j
        conv = conv + conv_ref[j:j + 1, :] * xbuf_ref[off:off + t_blk, :]
    xbuf_ref[0:SUBLANES, :] = xbuf_ref[t_blk:t_blk + SUBLANES, :]
    xc = _silu(conv).astype(BF16)
    for h in range(HEADS):
        cols = slice(ML_DH * h, ML_DH * (h + 1))
        q_ref[:, cols] = _dot(xc[:, cols], wq_ref[h]).astype(BF16)
        kt_ref[h] = _dot_nt(wkt_ref[h], xc[:, cols]) * ML_DH ** -0.5

    gates = g_ref[...] + bias_ref[...]
    ig = gates
    lf = _log_sigmoid(pltpu.roll(gates, 128 - HEADS, 1))
    row_in_chunk = lax.broadcasted_iota(jnp.int32, (t_blk, 128), 0) % CHUNK
    bcum = _segmented_scan(lf, row_in_chunk, jnp.add, 0.0)
    dgate = ig - bcum
    cmax = _segmented_scan(dgate, row_in_chunk, jnp.maximum, -jnp.inf)
    m_loc = bcum + cmax
    m_prev = m_ref[0:1, :]
    w_intra, w_inter, floor, wk_t, dec_scale = [], [], [], [], []
    for c in range(n_chunks):
        rows = slice(CHUNK * c, CHUNK * (c + 1))
        last = slice(CHUNK * (c + 1) - 1, CHUNK * (c + 1))
        b_last = bcum[last]
        g_max = b_last + cmax[last]
        m_new = jnp.maximum(b_last + m_prev, g_max)
        dec_scale.append(jnp.exp(b_last + m_prev - m_new))
        dec_scale.append(jnp.exp(g_max - m_new))
        inter = bcum[rows] + m_prev
        m_r = jnp.maximum(m_loc[rows], inter)
        w_inter.append(jnp.exp(inter - m_r))
        w_intra.append(jnp.exp(m_loc[rows] - m_r))
        floor.append(jnp.exp(-m_r))
        wk_t.append(jnp.exp(b_last - bcum[rows] + ig[rows] - g_max))
        m_prev = m_new
    m_ref[0:1, :] = m_prev
    pad_rows = -2 * n_chunks % SUBLANES
    if pad_rows:
        dec_scale.append(jnp.zeros((pad_rows, 128), F32))
    per_time = [cmax, jnp.concatenate(w_intra, axis=0), jnp.concatenate(w_inter, axis=0),
                jnp.concatenate(floor, axis=0), jnp.concatenate(dec_scale, axis=0)]
    offs = [0]
    for t in per_time:
        offs.append(offs[-1] + t.shape[0])
    for h in range(HEADS):
        for t, o in zip(per_time, offs):
            rep_ref[h, o:o + t.shape[0], :] = jnp.broadcast_to(t[:, h:h + 1], t.shape)

    r_i = lax.broadcasted_iota(jnp.int32, (CHUNK, CHUNK), 0)
    c_i = lax.broadcasted_iota(jnp.int32, (CHUNK, CHUNK), 1)
    causal = c_i <= r_i
    ones = jnp.ones((CHUNK, ML_DH), BF16)
    for c in range(n_chunks):
        rows = slice(CHUNK * c, CHUNK * (c + 1))
        d_t = _time_on_lanes(dgate[rows])
        wk_tt = _time_on_lanes(wk_t[c])
        for h in range(HEADS):
            cols = slice(ML_DH * h, ML_DH * (h + 1))
            rep = lambda i: rep_ref[h, offs[i] + CHUNK * c:offs[i] + CHUNK * (c + 1), :]
            cmax_b, a_b, i_b, floor_b = rep(0), rep(1), rep(2), rep(3)
            dec_b = rep_ref[h, offs[4] + 2 * c:offs[4] + 2 * c + 1, :]
            scale_b = rep_ref[h, offs[4] + 2 * c + 1:offs[4] + 2 * c + 2, :]
            qb = q_ref[rows, cols]
            kt = kt_ref[h, :, rows]
            v_ext = jnp.concatenate(
                [p_ref[rows, GROUP_W + ML_DH * h:GROUP_W + ML_DH * (h + 1)].astype(BF16), ones], axis=1)
            cn_prev = cn_ref[h]

            w = jnp.exp(jnp.where(causal, d_t[h:h + 1, 0:CHUNK] - cmax_b[:, 0:CHUNK], -jnp.inf))
            s = _dot(qb, kt.astype(BF16)) * w
            intra = _dot(s.astype(BF16), v_ext)
            inter = _dot(qb, cn_prev.astype(BF16))
            both = jnp.concatenate([a_b, a_b], axis=1) * intra + jnp.concatenate([i_b, i_b], axis=1) * inter
            hh = both[:, 0:ML_DH] / jnp.maximum(jnp.abs(both[:, ML_DH:]), floor_b)

            ktw = (kt * wk_tt[h:h + 1, 0:CHUNK]).astype(BF16)
            cn_ref[h] = (jnp.concatenate([dec_b, dec_b], axis=1) * cn_prev
                         + jnp.concatenate([scale_b, scale_b], axis=1) * _dot(ktw, v_ext))

            og = p_ref[rows, 2 * GROUP_W + ML_DH * h:2 * GROUP_W + ML_DH * (h + 1)]
            z = p_ref[rows, 3 * GROUP_W + ML_DH * h:3 * GROUP_W + ML_DH * (h + 1)]
            o_ref[rows, cols] = (_norm_rows(hh * _sigmoid(og), GN_EPS) * gn_ref[:, cols] * _silu(z))


def _mlstm(p_ml, p_if, w, layer, bsz, seq):
    conv_w, wq, wkt, bias, gn = w
    t_blk = min(ML_BLOCK_CHUNKS * CHUNK, seq)
    nb = seq // t_blk
    n_chunks = t_blk // CHUNK
    rep_rows = 4 * t_blk + 2 * n_chunks + (-2 * n_chunks % SUBLANES)
    row = lambda b, c: (b * nb + c, 0)
    return pl.pallas_call(
        _ml_kernel,
        grid=(bsz, nb),
        in_specs=[pl.BlockSpec((t_blk, W_ML), row),
                  pl.BlockSpec((t_blk, W_IF), row),
                  _layer_spec(conv_w, layer),
                  _layer_spec(wq, layer),
                  _layer_spec(wkt, layer),
                  _layer_spec(bias, layer),
                  _layer_spec(gn, layer)],
        out_specs=pl.BlockSpec((t_blk, GROUP_W), row),
        out_shape=jax.ShapeDtypeStruct((bsz * seq, GROUP_W), F32),
        scratch_shapes=[pltpu.VMEM((t_blk + SUBLANES, GROUP_W), F32),
                        pltpu.VMEM((t_blk, GROUP_W), BF16),
                        pltpu.VMEM((HEADS, ML_DH, t_blk), F32),
                        pltpu.VMEM((HEADS, rep_rows, 128), F32),
                        pltpu.VMEM((HEADS, ML_DH, 2 * ML_DH), F32),
                        pltpu.VMEM((SUBLANES, 128), F32)],
        compiler_params=_params(("parallel", "arbitrary")),
        name="mlstm",
    )(p_ml, p_if, conv_w, wq, wkt, bias, gn)


def _xa_kernel(p_ref, mk_ref, mv_ref, o_ref):
    z = p_ref[:, GROUP_W:]
    outs = []
    for h in range(HEADS):
        cols = slice(XA_DH * h, XA_DH * (h + 1))
        s = _dot_nt(p_ref[:, cols].astype(BF16), mk_ref[:, cols]) * XA_DH ** -0.5
        e = jnp.exp(s - jnp.max(s, axis=1, keepdims=True))
        outs.append(_dot(e.astype(BF16), mv_ref[:, cols]) / jnp.sum(e, axis=1, keepdims=True))
    o_ref[...] = jnp.concatenate(outs, axis=1) * _silu(z)


def _cross_attention(p_xa, mk, mv, bsz, seq):
    t_blk = min(256, seq)
    nb = seq // t_blk
    row = lambda b, j: (b * nb + j, 0)
    mem = lambda b, j: (b, 0)
    return pl.pallas_call(
        _xa_kernel,
        grid=(bsz, nb),
        in_specs=[pl.BlockSpec((t_blk, W_XA), row),
                  pl.BlockSpec((N_MEM, GROUP_W), mem),
                  pl.BlockSpec((N_MEM, GROUP_W), mem)],
        out_specs=pl.BlockSpec((t_blk, GROUP_W), row),
        out_shape=jax.ShapeDtypeStruct((bsz * seq, GROUP_W), F32),
        compiler_params=_params(("parallel", "parallel")),
        name="memory_attention",
    )(p_xa, mk, mv)


N_RET_IN, N_ML_IN, N_XA_IN = 8, 7, 3


def _mixers_kernel(*refs):
    i0 = 0
    ret_in = refs[i0:i0 + N_RET_IN]; i0 += N_RET_IN
    ml_in = refs[i0:i0 + N_ML_IN]; i0 += N_ML_IN
    xa_in = refs[i0:i0 + N_XA_IN]; i0 += N_XA_IN
    o_ret, o_ml, o_xa = refs[i0:i0 + 3]; i0 += 3
    ret_scratch = refs[i0:i0 + 1]; i0 += 1
    ml_scratch = refs[i0:]
    _xa_kernel(*xa_in, o_xa)
    _ret_kernel(*ret_in, o_ret, *ret_scratch)
    _ml_kernel(*ml_in, o_ml, *ml_scratch)


def _mixers(p_ret, cos, sin, tabs, p_ml, p_if, ml_w, layer, p_xa, mk, mv, bsz, seq):
    t_blk = min(ML_BLOCK_CHUNKS * CHUNK, seq)
    nb = seq // t_blk
    n_chunks = t_blk // CHUNK
    rep_rows = 4 * t_blk + 2 * n_chunks + (-2 * n_chunks % SUBLANES)
    row = lambda b, c: (b * nb + c, 0)
    mem = lambda b, c: (b, 0)
    const = lambda t: pl.BlockSpec(t.shape, lambda b, c, nd=t.ndim: (0,) * nd)
    out = jax.ShapeDtypeStruct((bsz * seq, GROUP_W), F32)
    return pl.pallas_call(
        _mixers_kernel,
        grid=(bsz, nb),
        in_specs=[pl.BlockSpec((t_blk, W_RET), row),
                  pl.BlockSpec((t_blk, 128), row),
                  pl.BlockSpec((t_blk, 128), row)] + [const(t) for t in tabs] +
                 [pl.BlockSpec((t_blk, W_ML), row),
                  pl.BlockSpec((t_blk, W_IF), row)] + [_layer_spec(t, layer) for t in ml_w] +
                 [pl.BlockSpec((t_blk, W_XA), row),
                  pl.BlockSpec((N_MEM, GROUP_W), mem),
                  pl.BlockSpec((N_MEM, GROUP_W), mem)],
        out_specs=[pl.BlockSpec((t_blk, GROUP_W), row)] * 3,
        out_shape=[out] * 3,
        scratch_shapes=[pltpu.VMEM((HEADS * RET_DK, HEADS * RET_DV), F32),
                        pltpu.VMEM((t_blk + SUBLANES, GROUP_W), F32),
                        pltpu.VMEM((t_blk, GROUP_W), BF16),
                        pltpu.VMEM((HEADS, ML_DH, t_blk), F32),
                        pltpu.VMEM((HEADS, rep_rows, 128), F32),
                        pltpu.VMEM((HEADS, ML_DH, 2 * ML_DH), F32),
                        pltpu.VMEM((SUBLANES, 128), F32)],
        compiler_params=_params(("parallel", "arbitrary")),
        name="mixers",
    )(p_ret, cos, sin, *tabs, p_ml, p_if, *ml_w, p_xa, mk, mv)


def _out_kernel(x_ref, r_ref, s_ref, m_ref, a_ref, w_ref, g_ref, b_ref, o_ref):
    y = None
    s5o = jnp.concatenate(
        [jnp.concatenate([s_ref[j, t] for t in range(s_ref.shape[1])], axis=0)
         for j in range(s_ref.shape[0])], axis=1)
    for i, t in enumerate((r_ref[...], s5o, m_ref[...], a_ref[...])):
        part = _dot(t.astype(BF16), w_ref[GROUP_W * i:GROUP_W * (i + 1), :])
        y = part if y is None else y + part
    t = ALPHA * x_ref[...] + y
    o_ref[...] = _norm_rows(t, LN_EPS) * g_ref[...] + b_ref[...]


def _out_project(x2, groups, w_out, ln_g, ln_b, layer, seq):
    m = x2.shape[0]
    tm = min(PROJ_ROWS, seq)
    nj = seq // tm
    row = lambda i: (i, 0)
    return pl.pallas_call(
        _out_kernel,
        grid=(m // tm,),
        in_specs=[pl.BlockSpec((tm, D_MODEL), row),
                  pl.BlockSpec((tm, GROUP_W), row),
                  pl.BlockSpec((GROUP_W // 128, tm // S5_TIME_BLOCK, S5_TIME_BLOCK, 128),
                               lambda i: (0, i % nj, i // nj, 0)),
                  pl.BlockSpec((tm, GROUP_W), row),
                  pl.BlockSpec((tm, GROUP_W), row),
                  _layer_spec(w_out, layer),
                  _layer_spec(ln_g, layer),
                  _layer_spec(ln_b, layer)],
        out_specs=pl.BlockSpec((tm, D_MODEL), row),
        out_shape=jax.ShapeDtypeStruct((m, D_MODEL), F32),
        compiler_params=_params(("parallel",)),
        name="out_proj_norm",
    )(x2, *groups, w_out, ln_g, ln_b)


def _split_w_in(w_in):
    half = RET_DK // 2
    perm = jnp.array([h * RET_DK + s * half + j for s in range(2) for h in range(HEADS)
                      for j in range(half)], jnp.int32)
    n_qk = HEADS * RET_DK
    w_qk = jnp.concatenate([w_in[:, :n_qk][:, perm],
                            w_in[:, n_qk:2 * n_qk][:, perm] * RET_DK ** -0.5],
                           axis=1).astype(BF16)
    w_xa = w_in[:, C_AQ:C_AQ + W_XA].astype(BF16)
    w_if = jnp.pad(w_in[:, C_MI:C_AQ], ((0, 0), (0, W_IF - 2 * HEADS))).astype(BF16)
    return w_in[:, :C_MI].astype(BF16), w_qk, w_xa, w_if


def _prepare_weights(w_in, s5_a_re, s5_a_im, s5_log_dt, s5_b_re, s5_b_im, s5_c_re, s5_c_im,
                     s5_d, s5_w_glu, ml_conv_w, ml_wq, ml_wk, ml_b_ig, ml_b_fg, ml_gn, xa_w_kv, w_out,
                     ln_g, ln_b):
    bias = jnp.concatenate([ml_b_ig, ml_b_fg, jnp.zeros((W_IF - 2 * HEADS,), F32)])[None, :]
    return dict(
        w_in=_split_w_in(w_in),
        s5=_s5_weights(s5_a_re, s5_a_im, s5_log_dt, s5_b_re, s5_b_im, s5_c_re, s5_c_im, s5_d,
                       s5_w_glu),
        ml=(ml_conv_w, ml_wq.astype(BF16), jnp.swapaxes(ml_wk, 1, 2).astype(BF16), bias,
            ml_gn[None, :]),
        w_kv=xa_w_kv.astype(BF16), w_out=w_out.astype(BF16), ln_g=ln_g[None, :], ln_b=ln_b[None, :])


def _layer(x2, mem2, cos, sin, ret_tabs, bsz, seq, w, layer):
    p_ml, p_s5, p_xa, p_ret, p_if = _project(x2, w["w_in"], layer, bsz, seq)
    s5o = _s5(p_s5, w["s5"], layer, bsz, seq)
    mk, mv = _memory_kv(mem2, w["w_kv"], layer)
    ret, ml, xa = _mixers(p_ret, cos, sin, ret_tabs, p_ml, p_if, w["ml"], layer, p_xa, mk, mv,
                          bsz, seq)
    return _out_project(x2, (ret, s5o, ml, xa), w["w_out"], w["ln_g"], w["ln_b"], layer, seq)


def kernel(x, mem, positions, w_in, s5_a_re, s5_a_im, s5_log_dt, s5_b_re, s5_b_im, s5_c_re, s5_c_im,
           s5_d, s5_w_glu, ml_conv_w, ml_wq, ml_wk, ml_b_ig, ml_b_fg, ml_gn, xa_w_kv, w_out, ln_g,
           ln_b):
    bsz, seq, _ = x.shape
    x2 = x.reshape(bsz * seq, D_MODEL)
    mem2 = mem.reshape(bsz * N_MEM, D_MODEL)
    cos, sin = _rope_tables(positions)
    ret_tabs = _retention_tables()
    weights = jax.vmap(_prepare_weights)(
        w_in, s5_a_re, s5_a_im, s5_log_dt, s5_b_re, s5_b_im, s5_c_re, s5_c_im, s5_d, s5_w_glu,
        ml_conv_w, ml_wq, ml_wk, ml_b_ig, ml_b_fg, ml_gn, xa_w_kv, w_out, ln_g, ln_b)
    for l in range(w_in.shape[0]):
        x2 = _layer(x2, mem2, cos, sin, ret_tabs, bsz, seq, weights, l)
    return x2.reshape(bsz, seq, D_MODEL)
```

```python
import functools
import math

import jax
import jax.numpy as jnp
from jax import lax
from jax.experimental import pallas as pl
from jax.experimental.pallas import tpu as pltpu

F32 = jnp.float32
BF16 = jnp.bfloat16

D_MODEL = 1024
CHUNK = 64
N_MEM = 256
GROUP_W = 512
HEADS = 4
RET_DK = 64
RET_DV = 128
ROPE_BASE = 10000.0
S5_CH = 16
S5_GROUPS = 32
S5_STATE = 64
S5_LANES = S5_GROUPS * S5_STATE
ML_DH = 128
ML_CONV = 4
XA_DH = 128
LN_EPS = 1e-5
GN_EPS = 1e-6
DEPTH = 4
ALPHA = (2 * DEPTH) ** 0.25

SUBLANES = 8
VMEM_LIMIT = 56 * 1024 * 1024

W_ML, W_S5, W_XA, W_RET, W_IF = 4 * GROUP_W, 2 * GROUP_W, 2 * GROUP_W, 3 * GROUP_W, 128


def _params(sem):
    return pltpu.CompilerParams(dimension_semantics=sem, vmem_limit_bytes=VMEM_LIMIT)


def _sigmoid(x):
    return 1.0 / (1.0 + jnp.exp(-x))


def _silu(x):
    return x * _sigmoid(x)


def _norm_rows(t, eps):
    mu = jnp.mean(t, axis=-1, keepdims=True)
    d = t - mu
    var = jnp.mean(d * d, axis=-1, keepdims=True)
    return d * lax.rsqrt(var + eps)


def _layer_spec(t, layer, **kw):
    return pl.BlockSpec((None,) + t.shape[1:], lambda *_: (layer,) + (0,) * (t.ndim - 1), **kw)


def _dot(a, b):
    return jnp.dot(a, b, preferred_element_type=F32)


def _dot_nt(a, b):
    return lax.dot_general(a, b, (((1,), (1,)), ((), ())), preferred_element_type=F32)


def _dot_tn(a, b):
    return lax.dot_general(a, b, (((0,), (0,)), ((), ())), preferred_element_type=F32)


def _rope_kernel(pos_ref, inv_ref, cos_ref, sin_ref):
    ang = pos_ref[...].astype(F32) * inv_ref[...]
    cos_ref[...] = jnp.cos(ang)
    sin_ref[...] = jnp.sin(ang)


def _rope_tables(positions):
    m = positions.size
    half = RET_DK // 2
    inv = ROPE_BASE ** (-jnp.arange(half, dtype=F32) / half)
    inv = jnp.tile(inv, HEADS)[None, :]
    tm = min(512, m)
    return pl.pallas_call(
        _rope_kernel,
        grid=(m // tm,),
        in_specs=[pl.BlockSpec((tm, 1), lambda i: (i, 0)),
                  pl.BlockSpec((1, 128), lambda i: (0, 0))],
        out_specs=[pl.BlockSpec((tm, 128), lambda i: (i, 0))] * 2,
        out_shape=[jax.ShapeDtypeStruct((m, 128), F32)] * 2,
        compiler_params=_params(("parallel",)),
        name="rope_tables",
    )(positions.reshape(m, 1), inv)


PROJ_ROWS = 512
S5_TIME_BLOCK = CHUNK

C_RV, C_SU, C_MX, C_MI, C_AQ = 512, 1536, 2560, 4608, 4616


def _proj_kernel(x_ref, w_ref, wqk_ref, wxa_ref, wif_ref, o_ml, o_s5, o_xa, o_ret, o_if):
    xb = x_ref[...].astype(BF16)
    o_ml[...] = _dot(xb, w_ref[:, C_MX:C_MX + W_ML])
    tb = S5_TIME_BLOCK
    for j in range(W_S5 // 256):
        r = _dot(xb, w_ref[:, C_SU + 256 * j:C_SU + 256 * (j + 1)])
        for half in range(2):
            for t in range(o_s5.shape[1]):
                o_s5[2 * j + half, t] = r[tb * t:tb * (t + 1), 128 * half:128 * (half + 1)]
    o_xa[...] = _dot(xb, wxa_ref[...])
    o_ret[:, 0:2 * HEADS * RET_DK] = _dot(xb, wqk_ref[...])
    o_ret[:, 2 * HEADS * RET_DK:] = _dot(xb, w_ref[:, C_RV:C_SU])
    o_if[...] = _dot(xb, wif_ref[...])


def _project(x2, w, layer, bsz, seq):
    w_main, w_qk, w_xa, w_if = w
    m = x2.shape[0]
    tm = min(PROJ_ROWS, seq)
    nj = seq // tm
    tpb = tm // S5_TIME_BLOCK
    row = lambda i: (i, 0)
    s5_shape = (W_S5 // 128, seq // S5_TIME_BLOCK, bsz * S5_TIME_BLOCK, 128)
    specs = [pl.BlockSpec((tm, W_ML), row),
             pl.BlockSpec((W_S5 // 128, tpb, S5_TIME_BLOCK, 128),
                          lambda i: (0, i % nj, i // nj, 0)),
             pl.BlockSpec((tm, W_XA), row),
             pl.BlockSpec((tm, W_RET), row),
             pl.BlockSpec((tm, W_IF), row)]
    shapes = [(m, W_ML), s5_shape, (m, W_XA), (m, W_RET), (m, W_IF)]
    return pl.pallas_call(
        _proj_kernel,
        grid=(m // tm,),
        in_specs=[pl.BlockSpec((tm, D_MODEL), row)] +
                 [_layer_spec(t, layer, pipeline_mode=pl.Buffered(1))
                  for t in (w_main, w_qk, w_xa, w_if)],
        out_specs=specs,
        out_shape=[jax.ShapeDtypeStruct(sh, F32) for sh in shapes],
        compiler_params=_params(("parallel",)),
        name="in_proj",
    )(x2, w_main, w_qk, w_xa, w_if)


def _kv_kernel(m_ref, w_ref, k_ref, v_ref):
    mb = m_ref[...].astype(BF16)
    k_ref[...] = _dot(mb, w_ref[:, :GROUP_W]).astype(BF16)
    v_ref[...] = _dot(mb, w_ref[:, GROUP_W:]).astype(BF16)


def _memory_kv(mem2, w_kv, layer):
    m = mem2.shape[0]
    return pl.pallas_call(
        _kv_kernel,
        grid=(m // N_MEM,),
        in_specs=[pl.BlockSpec((N_MEM, D_MODEL), lambda i: (i, 0)),
                  _layer_spec(w_kv, layer)],
        out_specs=[pl.BlockSpec((N_MEM, GROUP_W), lambda i: (i, 0))] * 2,
        out_shape=[jax.ShapeDtypeStruct((m, GROUP_W), BF16)] * 2,
        compiler_params=_params(("parallel",)),
        name="memory_kv",
    )(mem2, w_kv)


RET_BLOCK_CHUNKS = 4


def _ret_kernel(p_ref, cos_ref, sin_ref, hmask_ref, intra_ref, qdec_ref, kdec_ref, cdec_ref,
                o_ref, s_ref):
    @pl.when(pl.program_id(1) == 0)
    def _():
        s_ref[...] = jnp.zeros_like(s_ref)

    cs = cos_ref[...]
    sn = sin_ref[...]

    def rot(t):
        t1 = t[:, :128]
        t2 = t[:, 128:]
        return jnp.concatenate([t1 * cs - t2 * sn, t1 * sn + t2 * cs], axis=1)

    qr = rot(p_ref[:, 0:256])
    kr = rot(p_ref[:, 256:512])
    for c in range(p_ref.shape[0] // CHUNK):
        rows = slice(CHUNK * c, CHUNK * (c + 1))
        vb = p_ref[rows, 512:1024].astype(BF16)
        krb = kr[rows].astype(BF16)
        state = s_ref[...]
        sb = state.astype(BF16)
        for h in range(HEADS):
            cols = slice(RET_DV * h, RET_DV * (h + 1))
            qm = (qr[rows] * hmask_ref[h:h + 1, :]).astype(BF16)
            att = (_dot_nt(qm, krb) * intra_ref[h]).astype(BF16)
            o = _dot(att, vb[:, cols]) + qdec_ref[:, cols] * _dot(qm, sb[:, cols])
            z = p_ref[rows, 1024 + RET_DV * h:1024 + RET_DV * (h + 1)]
            o_ref[rows, cols] = _norm_rows(o, GN_EPS) * _silu(z)
        kd = (kr[rows] * kdec_ref[...]).astype(BF16)
        s_ref[...] = cdec_ref[...] * state + _dot_tn(kd, vb)


def _retention(p_ret, cos, sin, tabs, bsz, seq):
    t_blk = min(RET_BLOCK_CHUNKS * CHUNK, seq)
    nb = seq // t_blk
    row = lambda b, c: (b * nb + c, 0)
    const2 = lambda b, c: (0, 0)
    hmask, intra, qdec, kdec, cdec = tabs
    return pl.pallas_call(
        _ret_kernel,
        grid=(bsz, nb),
        in_specs=[pl.BlockSpec((t_blk, W_RET), row),
                  pl.BlockSpec((t_blk, 128), row),
                  pl.BlockSpec((t_blk, 128), row),
                  pl.BlockSpec(hmask.shape, const2),
                  pl.BlockSpec(intra.shape, lambda b, c: (0, 0, 0)),
                  pl.BlockSpec(qdec.shape, const2),
                  pl.BlockSpec(kdec.shape, const2),
                  pl.BlockSpec(cdec.shape, const2)],
        out_specs=pl.BlockSpec((t_blk, GROUP_W), row),
        out_shape=jax.ShapeDtypeStruct((bsz * seq, GROUP_W), F32),
        scratch_shapes=[pltpu.VMEM((HEADS * RET_DK, HEADS * RET_DV), F32)],
        compiler_params=_params(("parallel", "arbitrary")),
        name="retention",
    )(p_ret, cos, sin, hmask, intra, qdec, kdec, cdec)


def _retention_tables():
    log_g = jnp.log1p(-jnp.exp2(-5.0 - jnp.arange(HEADS, dtype=F32)))
    r = jnp.arange(CHUNK, dtype=F32)
    intra = jnp.exp(jnp.abs(r[:, None] - r[None, :]) * log_g[:, None, None])
    q_dec = jnp.exp((r + 1.0) * log_g[:, None])
    k_dec = jnp.exp((CHUNK - 1.0 - r) * log_g[:, None])
    c_dec = jnp.exp(CHUNK * log_g)
    lane_head = (jnp.arange(HEADS * RET_DK) % 128) // (RET_DK // 2)
    hmask = (lane_head[None, :] == jnp.arange(HEADS)[:, None]).astype(F32)
    qdec = jnp.repeat(q_dec.T, RET_DV, axis=1)
    kdec = k_dec.T[:, lane_head]
    cdec = jnp.repeat(c_dec, RET_DV)[None, :]
    return hmask, intra, qdec, kdec, cdec


def _gelu_tanh(x):
    return 0.5 * x * (1.0 + jnp.tanh(math.sqrt(2.0 / math.pi) * (x + 0.044715 * (x * x * x))))


S5_LANE_BLOCK = 512
S5_KTILE_GROUPS = 16
S5_KTILE_IN = S5_KTILE_GROUPS * S5_CH
S5_KTILE_LANES = S5_KTILE_GROUPS * S5_STATE
S5_KTILES = S5_GROUPS // S5_KTILE_GROUPS


def _s5_kernel(p_ref, bbd_ref, cbd_ref, a_ref, d_ref, wglu_ref, o_ref, bu_ref, xs_ref, carry_ref):
    @pl.when(pl.program_id(0) == 0)
    def _():
        carry_ref[...] = jnp.zeros_like(carry_ref)

    tb = S5_TIME_BLOCK

    def gather(j):
        blk = p_ref[j, 0].reshape(SUBLANES, tb, 128)
        return jnp.swapaxes(blk, 0, 1).reshape(tb * SUBLANES, 128)

    nblk = GROUP_W // 128
    u = jnp.concatenate([gather(j) for j in range(nblk)], axis=1)
    z = jnp.concatenate([gather(nblk + j) for j in range(nblk)], axis=1)
    ub = u.astype(BF16)
    kw = 2 * S5_KTILE_LANES
    for kt in range(S5_KTILES):
        bu_ref[:, kt * kw:(kt + 1) * kw] = _dot(ub[:, kt * S5_KTILE_IN:(kt + 1) * S5_KTILE_IN],
                                                bbd_ref[kt])

    lb = S5_LANE_BLOCK
    for kt in range(S5_KTILES):
        for blk in range(S5_KTILE_LANES // lb):
            re = slice(kt * kw + blk * lb, kt * kw + (blk + 1) * lb)
            im = slice(kt * kw + S5_KTILE_LANES + blk * lb, kt * kw + S5_KTILE_LANES + (blk + 1) * lb)
            par = slice(kt * S5_KTILE_LANES + blk * lb, kt * S5_KTILE_LANES + (blk + 1) * lb)
            ar = jnp.broadcast_to(a_ref[0:1, par], (SUBLANES, lb))
            ai = jnp.broadcast_to(a_ref[1:2, par], (SUBLANES, lb))

            def two_steps(i, carry):
                xr, xi = carry
                rows = pl.ds(pl.multiple_of(i * 2 * SUBLANES, 2 * SUBLANES), 2 * SUBLANES)
                b_r = bu_ref[rows, re]
                b_i = bu_ref[rows, im]
                x1r = ar * xr - ai * xi + b_r[0:SUBLANES]
                x1i = ar * xi + ai * xr + b_i[0:SUBLANES]
                x2r = ar * x1r - ai * x1i + b_r[SUBLANES:]
                x2i = ar * x1i + ai * x1r + b_i[SUBLANES:]
                xs_ref[rows, re] = jnp.concatenate([x1r, x2r], axis=0).astype(BF16)
                xs_ref[rows, im] = jnp.concatenate([x1i, x2i], axis=0).astype(BF16)
                return x2r, x2i

            xr, xi = lax.fori_loop(0, tb // 2, two_steps,
                                   (carry_ref[0, :, par], carry_ref[1, :, par]), unroll=4)
            carry_ref[0, :, par] = xr
            carry_ref[1, :, par] = xi

    y = jnp.concatenate([_dot(xs_ref[:, kt * kw:(kt + 1) * kw], cbd_ref[kt])
                         for kt in range(S5_KTILES)], axis=1) + d_ref[...] * u
    y = _gelu_tanh(y)
    y = y * _sigmoid(_dot(y.astype(BF16), wglu_ref[...]))
    res = y * _silu(z)
    for j in range(nblk):
        blk = res[:, 128 * j:128 * (j + 1)].reshape(tb, SUBLANES, 128)
        o_ref[j, 0] = jnp.swapaxes(blk, 0, 1).reshape(SUBLANES * tb, 128)


def _s5(p_s5, w, layer, bsz, seq):
    assert bsz == SUBLANES, "the S5 kernel puts one batch element on each sublane"
    nb = seq // S5_TIME_BLOCK
    rows = bsz * S5_TIME_BLOCK
    blk = lambda j: (0, j, 0, 0)
    return pl.pallas_call(
        _s5_kernel,
        grid=(nb,),
        in_specs=[pl.BlockSpec((W_S5 // 128, 1, rows, 128), blk)] +
                 [_layer_spec(t, layer) for t in w],
        out_specs=pl.BlockSpec((GROUP_W // 128, 1, rows, 128), blk),
        out_shape=jax.ShapeDtypeStruct((GROUP_W // 128, nb, rows, 128), F32),
        scratch_shapes=[pltpu.VMEM((rows, 2 * S5_LANES), F32),
                        pltpu.VMEM((rows, 2 * S5_LANES), BF16),
                        pltpu.VMEM((2, SUBLANES, S5_LANES), F32)],
        compiler_params=_params(("arbitrary",)),
        name="s5",
    )(p_s5, *w)


def _s5_weights(a_re, a_im, log_dt, b_re, b_im, c_re, c_im, d, w_glu):
    dt = jnp.exp(log_dt)[:, None]
    mag = jnp.exp(a_re * dt)
    ab_re = mag * jnp.cos(a_im * dt)
    ab_im = mag * jnp.sin(a_im * dt)
    den = a_re * a_re + a_im * a_im
    p = ab_re - 1.0
    f_re = (p * a_re + ab_im * a_im) / den
    f_im = (ab_im * a_re - p * a_im) / den
    bb_re = f_re[..., None] * b_re - f_im[..., None] * b_im
    bb_im = f_re[..., None] * b_im + f_im[..., None] * b_re
    kg = S5_KTILE_GROUPS
    own_group = (jnp.arange(kg)[:, None, None, None] == jnp.arange(kg)[None, None, :, None])

    def tiles_in(t):
        t = jnp.swapaxes(t, 1, 2).reshape(S5_KTILES, kg, S5_CH, 1, S5_STATE)
        return jnp.where(own_group, t, 0.0).reshape(S5_KTILES, S5_KTILE_IN, S5_KTILE_LANES)

    def tiles_out(t):
        t = jnp.swapaxes(t, 1, 2).reshape(S5_KTILES, kg, S5_STATE, 1, S5_CH)
        return jnp.where(own_group, t, 0.0).reshape(S5_KTILES, S5_KTILE_LANES, S5_KTILE_IN)

    bbd = jnp.concatenate([tiles_in(bb_re), tiles_in(bb_im)], axis=2).astype(BF16)
    cbd = jnp.concatenate([tiles_out(c_re), tiles_out(-c_im)], axis=1).astype(BF16)
    a = jnp.stack([ab_re.reshape(-1), ab_im.reshape(-1)])
    return bbd, cbd, a, d[None, :], w_glu.astype(BF16)


def _log_sigmoid(x):
    return jnp.minimum(x, 0.0) - jnp.log1p(jnp.exp(-jnp.abs(x)))


ML_BLOCK_CHUNKS = 4


def _segmented_scan(x, row_in_chunk, combine, identity):
    sh = 1
    while sh < CHUNK:
        x = combine(x, jnp.where(row_in_chunk >= sh, pltpu.roll(x, sh, 0), identity))
        sh *= 2
    return x


def _time_on_lanes(t):
    return jnp.concatenate([t, jnp.zeros_like(t)], axis=0).T


def _ml_kernel(p_ref, g_ref, conv_ref, wq_ref, wkt_ref, bias_ref, gn_ref, o_ref,
               xbuf_ref, q_ref, kt_ref, rep_ref, cn_ref, m_ref):
    t_blk = p_ref.shape[0]
    n_chunks = t_blk // CHUNK

    @pl.when(pl.program_id(1) == 0)
    def _():
        xbuf_ref[0:SUBLANES, :] = jnp.zeros((SUBLANES, GROUP_W), F32)
        cn_ref[...] = jnp.zeros_like(cn_ref)
        m_ref[...] = jnp.zeros_like(m_ref)

    xbuf_ref[SUBLANES:, :] = p_ref[:, 0:GROUP_W]
    conv = jnp.zeros((t_blk, GROUP_W), F32)
    for j in range(ML_CONV):
        off = SUBLANES - (ML_CONV - 1) + j
        conv = conv + conv_ref[j:j + 1, :] * xbuf_ref[off:off + t_blk, :]
    xbuf_ref[0:SUBLANES, :] = xbuf_ref[t_blk:t_blk + SUBLANES, :]
    xc = _silu(conv).astype(BF16)
    for h in range(HEADS):
        cols = slice(ML_DH * h, ML_DH * (h + 1))
        q_ref[:, cols] = _dot(xc[:, cols], wq_ref[h]).astype(BF16)
        kt_ref[h] = _dot_nt(wkt_ref[h], xc[:, cols]) * ML_DH ** -0.5

    gates = g_ref[...] + bias_ref[...]
    ig = gates
    lf = _log_sigmoid(pltpu.roll(gates, 128 - HEADS, 1))
    row_in_chunk = lax.broadcasted_iota(jnp.int32, (t_blk, 128), 0) % CHUNK
    bcum = _segmented_scan(lf, row_in_chunk, jnp.add, 0.0)
    dgate = ig - bcum
    cmax = _segmented_scan(dgate, row_in_chunk, jnp.maximum, -jnp.inf)
    m_loc = bcum + cmax
    m_prev = m_ref[0:1, :]
    w_intra, w_inter, floor, wk_t, dec_scale = [], [], [], [], []
    for c in range(n_chunks):
        rows = slice(CHUNK * c, CHUNK * (c + 1))
        last = slice(CHUNK * (c + 1) - 1, CHUNK * (c + 1))
        b_last = bcum[last]
        g_max = b_last + cmax[last]
        m_new = jnp.maximum(b_last + m_prev, g_max)
        dec_scale.append(jnp.exp(b_last + m_prev - m_new))
        dec_scale.append(jnp.exp(g_max - m_new))
        inter = bcum[rows] + m_prev
        m_r = jnp.maximum(m_loc[rows], inter)
        w_inter.append(jnp.exp(inter - m_r))
        w_intra.append(jnp.exp(m_loc[rows] - m_r))
        floor.append(jnp.exp(-m_r))
        wk_t.append(jnp.exp(b_last - bcum[rows] + ig[rows] - g_max))
        m_prev = m_new
    m_ref[0:1, :] = m_prev
    pad_rows = -2 * n_chunks % SUBLANES
    if pad_rows:
        dec_scale.append(jnp.zeros((pad_rows, 128), F32))
    per_time = [cmax, jnp.concatenate(w_intra, axis=0), jnp.concatenate(w_inter, axis=0),
                jnp.concatenate(floor, axis=0), jnp.concatenate(dec_scale, axis=0)]
    offs = [0]
    for t in per_time:
        offs.append(offs[-1] + t.shape[0])
    for h in range(HEADS):
        for t, o in zip(per_time, offs):
            rep_ref[h, o:o + t.shape[0], :] = jnp.broadcast_to(t[:, h:h + 1], t.shape)

    r_i = lax.broadcasted_iota(jnp.int32, (CHUNK, CHUNK), 0)
    c_i = lax.broadcasted_iota(jnp.int32, (CHUNK, CHUNK), 1)
    causal = c_i <= r_i
    ones = jnp.ones((CHUNK, ML_DH), BF16)
    for c in range(n_chunks):
        rows = slice(CHUNK * c, CHUNK * (c + 1))
        d_t = _time_on_lanes(dgate[rows])
        wk_tt = _time_on_lanes(wk_t[c])
        for h in range(HEADS):
            cols = slice(ML_DH * h, ML_DH * (h + 1))
            rep = lambda i: rep_ref[h, offs[i] + CHUNK * c:offs[i] + CHUNK * (c + 1), :]
            cmax_b, a_b, i_b, floor_b = rep(0), rep(1), rep(2), rep(3)
            dec_b = rep_ref[h, offs[4] + 2 * c:offs[4] + 2 * c + 1, :]
            scale_b = rep_ref[h, offs[4] + 2 * c + 1:offs[4] + 2 * c + 2, :]
            qb = q_ref[rows, cols]
            kt = kt_ref[h, :, rows]
            v_ext = jnp.concatenate(
                [p_ref[rows, GROUP_W + ML_DH * h:GROUP_W + ML_DH * (h + 1)].astype(BF16), ones], axis=1)
            cn_prev = cn_ref[h]

            w = jnp.exp(jnp.where(causal, d_t[h:h + 1, 0:CHUNK] - cmax_b[:, 0:CHUNK], -jnp.inf))
            s = _dot(qb, kt.astype(BF16)) * w
            intra = _dot(s.astype(BF16), v_ext)
            inter = _dot(qb, cn_prev.astype(BF16))
            both = jnp.concatenate([a_b, a_b], axis=1) * intra + jnp.concatenate([i_b, i_b], axis=1) * inter
            hh = both[:, 0:ML_DH] / jnp.maximum(jnp.abs(both[:, ML_DH:]), floor_b)

            ktw = (kt * wk_tt[h:h + 1, 0:CHUNK]).astype(BF16)
            cn_ref[h] = (jnp.concatenate([dec_b, dec_b], axis=1) * cn_prev
                         + jnp.concatenate([scale_b, scale_b], axis=1) * _dot(ktw, v_ext))

            og = p_ref[rows, 2 * GROUP_W + ML_DH * h:2 * GROUP_W + ML_DH * (h + 1)]
            z = p_ref[rows, 3 * GROUP_W + ML_DH * h:3 * GROUP_W + ML_DH * (h + 1)]
            o_ref[rows, cols] = (_norm_rows(hh * _sigmoid(og), GN_EPS) * gn_ref[:, cols] * _silu(z))


def _mlstm(p_ml, p_if, w, layer, bsz, seq):
    conv_w, wq, wkt, bias, gn = w
    t_blk = min(ML_BLOCK_CHUNKS * CHUNK, seq)
    nb = seq // t_blk
    n_chunks = t_blk // CHUNK
    rep_rows = 4 * t_blk + 2 * n_chunks + (-2 * n_chunks % SUBLANES)
    row = lambda b, c: (b * nb + c, 0)
    return pl.pallas_call(
        _ml_kernel,
        grid=(bsz, nb),
        in_specs=[pl.BlockSpec((t_blk, W_ML), row),
                  pl.BlockSpec((t_blk, W_IF), row),
                  _layer_spec(conv_w, layer),
                  _layer_spec(wq, layer),
                  _layer_spec(wkt, layer),
                  _layer_spec(bias, layer),
                  _layer_spec(gn, layer)],
        out_specs=pl.BlockSpec((t_blk, GROUP_W), row),
        out_shape=jax.ShapeDtypeStruct((bsz * seq, GROUP_W), F32),
        scratch_shapes=[pltpu.VMEM((t_blk + SUBLANES, GROUP_W), F32),
                        pltpu.VMEM((t_blk, GROUP_W), BF16),
                        pltpu.VMEM((HEADS, ML_DH, t_blk), F32),
                        pltpu.VMEM((HEADS, rep_rows, 128), F32),
                        pltpu.VMEM((HEADS, ML_DH, 2 * ML_DH), F32),
                        pltpu.VMEM((SUBLANES, 128), F32)],
        compiler_params=_params(("parallel", "arbitrary")),
        name="mlstm",
    )(p_ml, p_if, conv_w, wq, wkt, bias, gn)


def _xa_kernel(p_ref, mk_ref, mv_ref, o_ref):
    z = p_ref[:, GROUP_W:]
    outs = []
    for h in range(HEADS):
        cols = slice(XA_DH * h, XA_DH * (h + 1))
        s = _dot_nt(p_ref[:, cols].astype(BF16), mk_ref[:, cols]) * XA_DH ** -0.5
        e = jnp.exp(s - jnp.max(s, axis=1, keepdims=True))
        outs.append(_dot(e.astype(BF16), mv_ref[:, cols]) / jnp.sum(e, axis=1, keepdims=True))
    o_ref[...] = jnp.concatenate(outs, axis=1) * _silu(z)


def _cross_attention(p_xa, mk, mv, bsz, seq):
    t_blk = min(256, seq)
    nb = seq // t_blk
    row = lambda b, j: (b * nb + j, 0)
    mem = lambda b, j: (b, 0)
    return pl.pallas_call(
        _xa_kernel,
        grid=(bsz, nb),
        in_specs=[pl.BlockSpec((t_blk, W_XA), row),
                  pl.BlockSpec((N_MEM, GROUP_W), mem),
                  pl.BlockSpec((N_MEM, GROUP_W), mem)],
        out_specs=pl.BlockSpec((t_blk, GROUP_W), row),
        out_shape=jax.ShapeDtypeStruct((bsz * seq, GROUP_W), F32),
        compiler_params=_params(("parallel", "parallel")),
        name="memory_attention",
    )(p_xa, mk, mv)


N_RET_IN, N_ML_IN, N_XA_IN = 8, 7, 3


def _mixers_kernel(*refs):
    i0 = 0
    ret_in = refs[i0:i0 + N_RET_IN]; i0 += N_RET_IN
    ml_in = refs[i0:i0 + N_ML_IN]; i0 += N_ML_IN
    xa_in = refs[i0:i0 + N_XA_IN]; i0 += N_XA_IN
    o_ret, o_ml, o_xa = refs[i0:i0 + 3]; i0 += 3
    ret_scratch = refs[i0:i0 + 1]; i0 += 1
    ml_scratch = refs[i0:]
    _xa_kernel(*xa_in, o_xa)
    _ret_kernel(*ret_in, o_ret, *ret_scratch)
    _ml_kernel(*ml_in, o_ml, *ml_scratch)


def _mixers(p_ret, cos, sin, tabs, p_ml, p_if, ml_w, layer, p_xa, mk, mv, bsz, seq):
    t_blk = min(ML_BLOCK_CHUNKS * CHUNK, seq)
    nb = seq // t_blk
    n_chunks = t_blk // CHUNK
    rep_rows = 4 * t_blk + 2 * n_chunks + (-2 * n_chunks % SUBLANES)
    row = lambda b, c: (b * nb + c, 0)
    mem = lambda b, c: (b, 0)
    const = lambda t: pl.BlockSpec(t.shape, lambda b, c, nd=t.ndim: (0,) * nd)
    out = jax.ShapeDtypeStruct((bsz * seq, GROUP_W), F32)
    return pl.pallas_call(
        _mixers_kernel,
        grid=(bsz, nb),
        in_specs=[pl.BlockSpec((t_blk, W_RET), row),
                  pl.BlockSpec((t_blk, 128), row),
                  pl.BlockSpec((t_blk, 128), row)] + [const(t) for t in tabs] +
                 [pl.BlockSpec((t_blk, W_ML), row),
                  pl.BlockSpec((t_blk, W_IF), row)] + [_layer_spec(t, layer) for t in ml_w] +
                 [pl.BlockSpec((t_blk, W_XA), row),
                  pl.BlockSpec((N_MEM, GROUP_W), mem),
                  pl.BlockSpec((N_MEM, GROUP_W), mem)],
        out_specs=[pl.BlockSpec((t_blk, GROUP_W), row)] * 3,
        out_shape=[out] * 3,
        scratch_shapes=[pltpu.VMEM((HEADS * RET_DK, HEADS * RET_DV), F32),
                        pltpu.VMEM((t_blk + SUBLANES, GROUP_W), F32),
                        pltpu.VMEM((t_blk, GROUP_W), BF16),
                        pltpu.VMEM((HEADS, ML_DH, t_blk), F32),
                        pltpu.VMEM((HEADS, rep_rows, 128), F32),
                        pltpu.VMEM((HEADS, ML_DH, 2 * ML_DH), F32),
                        pltpu.VMEM((SUBLANES, 128), F32)],
        compiler_params=_params(("parallel", "arbitrary")),
        name="mixers",
    )(p_ret, cos, sin, *tabs, p_ml, p_if, *ml_w, p_xa, mk, mv)


def _out_kernel(x_ref, r_ref, s_ref, m_ref, a_ref, w_ref, g_ref, b_ref, o_ref):
    y = None
    s5o = jnp.concatenate(
        [jnp.concatenate([s_ref[j, t] for t in range(s_ref.shape[1])], axis=0)
         for j in range(s_ref.shape[0])], axis=1)
    for i, t in enumerate((r_ref[...], s5o, m_ref[...], a_ref[...])):
        part = _dot(t.astype(BF16), w_ref[GROUP_W * i:GROUP_W * (i + 1), :])
        y = part if y is None else y + part
    t = ALPHA * x_ref[...] + y
    o_ref[...] = _norm_rows(t, LN_EPS) * g_ref[...] + b_ref[...]


def _out_project(x2, groups, w_out, ln_g, ln_b, layer, seq):
    m = x2.shape[0]
    tm = min(PROJ_ROWS, seq)
    nj = seq // tm
    row = lambda i: (i, 0)
    return pl.pallas_call(
        _out_kernel,
        grid=(m // tm,),
        in_specs=[pl.BlockSpec((tm, D_MODEL), row),
                  pl.BlockSpec((tm, GROUP_W), row),
                  pl.BlockSpec((GROUP_W // 128, tm // S5_TIME_BLOCK, S5_TIME_BLOCK, 128),
                               lambda i: (0, i % nj, i // nj, 0)),
                  pl.BlockSpec((tm, GROUP_W), row),
                  pl.BlockSpec((tm, GROUP_W), row),
                  _layer_spec(w_out, layer),
                  _layer_spec(ln_g, layer),
                  _layer_spec(ln_b, layer)],
        out_specs=pl.BlockSpec((tm, D_MODEL), row),
        out_shape=jax.ShapeDtypeStruct((m, D_MODEL), F32),
        compiler_params=_params(("parallel",)),
        name="out_proj_norm",
    )(x2, *groups, w_out, ln_g, ln_b)


def _split_w_in(w_in):
    half = RET_DK // 2
    perm = jnp.array([h * RET_DK + s * half + j for s in range(2) for h in range(HEADS)
                      for j in range(half)], jnp.int32)
    n_qk = HEADS * RET_DK
    w_qk = jnp.concatenate([w_in[:, :n_qk][:, perm],
                            w_in[:, n_qk:2 * n_qk][:, perm] * RET_DK ** -0.5],
                           axis=1).astype(BF16)
    w_xa = w_in[:, C_AQ:C_AQ + W_XA].astype(BF16)
    w_if = jnp.pad(w_in[:, C_MI:C_AQ], ((0, 0), (0, W_IF - 2 * HEADS))).astype(BF16)
    return w_in[:, :C_MI].astype(BF16), w_qk, w_xa, w_if


def _prepare_weights(w_in, s5_a_re, s5_a_im, s5_log_dt, s5_b_re, s5_b_im, s5_c_re, s5_c_im,
                     s5_d, s5_w_glu, ml_conv_w, ml_wq, ml_wk, ml_b_ig, ml_b_fg, ml_gn, xa_w_kv, w_out,
                     ln_g, ln_b):
    bias = jnp.concatenate([ml_b_ig, ml_b_fg, jnp.zeros((W_IF - 2 * HEADS,), F32)])[None, :]
    return dict(
        w_in=_split_w_in(w_in),
        s5=_s5_weights(s5_a_re, s5_a_im, s5_log_dt, s5_b_re, s5_b_im, s5_c_re, s5_c_im, s5_d,
                       s5_w_glu),
        ml=(ml_conv_w, ml_wq.astype(BF16), jnp.swapaxes(ml_wk, 1, 2).astype(BF16), bias,
            ml_gn[None, :]),
        w_kv=xa_w_kv.astype(BF16), w_out=w_out.astype(BF16), ln_g=ln_g[None, :], ln_b=ln_b[None, :])


def _layer(x2, mem2, cos, sin, ret_tabs, bsz, seq, w, layer):
    p_ml, p_s5, p_xa, p_ret, p_if = _project(x2, w["w_in"], layer, bsz, seq)
    s5o = _s5(p_s5, w["s5"], layer, bsz, seq)
    mk, mv = _memory_kv(mem2, w["w_kv"], layer)
    ret, ml, xa = _mixers(p_ret, cos, sin, ret_tabs, p_ml, p_if, w["ml"], layer, p_xa, mk, mv,
                          bsz, seq)
    return _out_project(x2, (ret, s5o, ml, xa), w["w_out"], w["ln_g"], w["ln_b"], layer, seq)


def kernel(x, mem, positions, w_in, s5_a_re, s5_a_im, s5_log_dt, s5_b_re, s5_b_im, s5_c_re, s5_c_im,
           s5_d, s5_w_glu, ml_conv_w, ml_wq, ml_wk, ml_b_ig, ml_b_fg, ml_gn, xa_w_kv, w_out, ln_g,
           ln_b):
    bsz, seq, _ = x.shape
    x2 = x.reshape(bsz * seq, D_MODEL)
    mem2 = mem.reshape(bsz * N_MEM, D_MODEL)
    cos, sin = _rope_tables(positions)
    ret_tabs = _retention_tables()
    weights = jax.vmap(_prepare_weights)(
        w_in, s5_a_re, s5_a_im, s5_log_dt, s5_b_re, s5_b_im, s5_c_re, s5_c_im, s5_d, s5_w_glu,
        ml_conv_w, ml_wq, ml_wk, ml_b_ig, ml_b_fg, ml_gn, xa_w_kv, w_out, ln_g, ln_b)
    for l in range(w_in.shape[0]):
        x2 = _layer(x2, mem2, cos, sin, ret_tabs, bsz, seq, weights, l)
    return x2.reshape(bsz, seq, D_MODEL)
```

```python
import functools
import math

import jax
import jax.numpy as jnp
from jax import lax
from jax.experimental import pallas as pl
from jax.experimental.pallas import tpu as pltpu

F32 = jnp.float32
BF16 = jnp.bfloat16

D_MODEL = 1024
CHUNK = 64
N_MEM = 256
GROUP_W = 512
HEADS = 4
RET_DK = 64
RET_DV = 128
ROPE_BASE = 10000.0
S5_CH = 16
S5_GROUPS = 32
S5_STATE = 64
S5_LANES = S5_GROUPS * S5_STATE
ML_DH = 128
ML_CONV = 4
XA_DH = 128
LN_EPS = 1e-5
GN_EPS = 1e-6
DEPTH = 4
ALPHA = (2 * DEPTH) ** 0.25

SUBLANES = 8
VMEM_LIMIT = 56 * 1024 * 1024

W_ML, W_S5, W_XA, W_RET, W_IF = 4 * GROUP_W, 2 * GROUP_W, 2 * GROUP_W, 3 * GROUP_W, 128


def _params(sem):
    return pltpu.CompilerParams(dimension_semantics=sem, vmem_limit_bytes=VMEM_LIMIT)


def _sigmoid(x):
    return 1.0 / (1.0 + jnp.exp(-x))


def _silu(x):
    return x * _sigmoid(x)


def _norm_rows(t, eps):
    mu = jnp.mean(t, axis=-1, keepdims=True)
    d = t - mu
    var = jnp.mean(d * d, axis=-1, keepdims=True)
    return d * lax.rsqrt(var + eps)


def _layer_spec(t, layer, **kw):
    return pl.BlockSpec((None,) + t.shape[1:], lambda *_: (layer,) + (0,) * (t.ndim - 1), **kw)


def _dot(a, b):
    return jnp.dot(a, b, preferred_element_type=F32)


def _dot_nt(a, b):
    return lax.dot_general(a, b, (((1,), (1,)), ((), ())), preferred_element_type=F32)


def _dot_tn(a, b):
    return lax.dot_general(a, b, (((0,), (0,)), ((), ())), preferred_element_type=F32)


def _rope_kernel(pos_ref, inv_ref, cos_ref, sin_ref):
    ang = pos_ref[...].astype(F32) * inv_ref[...]
    cos_ref[...] = jnp.cos(ang)
    sin_ref[...] = jnp.sin(ang)


def _rope_tables(positions):
    m = positions.size
    half = RET_DK // 2
    inv = ROPE_BASE ** (-jnp.arange(half, dtype=F32) / half)
    inv = jnp.tile(inv, HEADS)[None, :]
    tm = min(512, m)
    return pl.pallas_call(
        _rope_kernel,
        grid=(m // tm,),
        in_specs=[pl.BlockSpec((tm, 1), lambda i: (i, 0)),
                  pl.BlockSpec((1, 128), lambda i: (0, 0))],
        out_specs=[pl.BlockSpec((tm, 128), lambda i: (i, 0))] * 2,
        out_shape=[jax.ShapeDtypeStruct((m, 128), F32)] * 2,
        compiler_params=_params(("parallel",)),
        name="rope_tables",
    )(positions.reshape(m, 1), inv)


PROJ_ROWS = 512
S5_TIME_BLOCK = CHUNK

C_RV, C_SU, C_MX, C_MI, C_AQ = 512, 1536, 2560, 4608, 4616


def _proj_kernel(x_ref, w_ref, wqk_ref, wxa_ref, wif_ref, o_ml, o_s5, o_xa, o_ret, o_if):
    xb = x_ref[...].astype(BF16)
    o_ml[...] = _dot(xb, w_ref[:, C_MX:C_MX + W_ML])
    tb = S5_TIME_BLOCK
    for j in range(W_S5 // 256):
        r = _dot(xb, w_ref[:, C_SU + 256 * j:C_SU + 256 * (j + 1)])
        for half in range(2):
            for t in range(o_s5.shape[1]):
                o_s5[2 * j + half, t] = r[tb * t:tb * (t + 1), 128 * half:128 * (half + 1)]
    o_xa[...] = _dot(xb, wxa_ref[...])
    o_ret[:, 0:2 * HEADS * RET_DK] = _dot(xb, wqk_ref[...])
    o_ret[:, 2 * HEADS * RET_DK:] = _dot(xb, w_ref[:, C_RV:C_SU])
    o_if[...] = _dot(xb, wif_ref[...])


def _project(x2, w, layer, bsz, seq):
    w_main, w_qk, w_xa, w_if = w
    m = x2.shape[0]
    tm = min(PROJ_ROWS, seq)
    nj = seq // tm
    tpb = tm // S5_TIME_BLOCK
    row = lambda i: (i, 0)
    s5_shape = (W_S5 // 128, seq // S5_TIME_BLOCK, bsz * S5_TIME_BLOCK, 128)
    specs = [pl.BlockSpec((tm, W_ML), row),
             pl.BlockSpec((W_S5 // 128, tpb, S5_TIME_BLOCK, 128),
                          lambda i: (0, i % nj, i // nj, 0)),
             pl.BlockSpec((tm, W_XA), row),
             pl.BlockSpec((tm, W_RET), row),
             pl.BlockSpec((tm, W_IF), row)]
    shapes = [(m, W_ML), s5_shape, (m, W_XA), (m, W_RET), (m, W_IF)]
    return pl.pallas_call(
        _proj_kernel,
        grid=(m // tm,),
        in_specs=[pl.BlockSpec((tm, D_MODEL), row)] +
                 [_layer_spec(t, layer, pipeline_mode=pl.Buffered(1))
                  for t in (w_main, w_qk, w_xa, w_if)],
        out_specs=specs,
        out_shape=[jax.ShapeDtypeStruct(sh, F32) for sh in shapes],
        compiler_params=_params(("parallel",)),
        name="in_proj",
    )(x2, w_main, w_qk, w_xa, w_if)


def _kv_kernel(m_ref, w_ref, k_ref, v_ref):
    mb = m_ref[...].astype(BF16)
    k_ref[...] = _dot(mb, w_ref[:, :GROUP_W]).astype(BF16)
    v_ref[...] = _dot(mb, w_ref[:, GROUP_W:]).astype(BF16)


def _memory_kv(mem2, w_kv, layer):
    m = mem2.shape[0]
    return pl.pallas_call(
        _kv_kernel,
        grid=(m // N_MEM,),
        in_specs=[pl.BlockSpec((N_MEM, D_MODEL), lambda i: (i, 0)),
                  _layer_spec(w_kv, layer)],
        out_specs=[pl.BlockSpec((N_MEM, GROUP_W), lambda i: (i, 0))] * 2,
        out_shape=[jax.ShapeDtypeStruct((m, GROUP_W), BF16)] * 2,
        compiler_params=_params(("parallel",)),
        name="memory_kv",
    )(mem2, w_kv)


def _ret_kernel(p_ref, cos_ref, sin_ref, hmask_ref, intra_ref, qdec_ref, kdec_ref, cdec_ref,
                o_ref, s_ref):
    @pl.when(pl.program_id(1) == 0)
    def _():
        s_ref[...] = jnp.zeros_like(s_ref)

    cs = cos_ref[...]
    sn = sin_ref[...]

    def rot(t):
        t1 = t[:, :128]
        t2 = t[:, 128:]
        return jnp.concatenate([t1 * cs - t2 * sn, t1 * sn + t2 * cs], axis=1)

    qr = rot(p_ref[:, 0:256])
    kr = rot(p_ref[:, 256:512])
    for c in range(p_ref.shape[0] // CHUNK):
        rows = slice(CHUNK * c, CHUNK * (c + 1))
        vb = p_ref[rows, 512:1024].astype(BF16)
        krb = kr[rows].astype(BF16)
        state = s_ref[...]
        sb = state.astype(BF16)
        for h in range(HEADS):
            cols = slice(RET_DV * h, RET_DV * (h + 1))
            qm = (qr[rows] * hmask_ref[h:h + 1, :]).astype(BF16)
            att = (_dot_nt(qm, krb) * intra_ref[h]).astype(BF16)
            o = _dot(att, vb[:, cols]) + qdec_ref[:, cols] * _dot(qm, sb[:, cols])
            z = p_ref[rows, 1024 + RET_DV * h:1024 + RET_DV * (h + 1)]
            o_ref[rows, cols] = _norm_rows(o, GN_EPS) * _silu(z)
        kd = (kr[rows] * kdec_ref[...]).astype(BF16)
        s_ref[...] = cdec_ref[...] * state + _dot_tn(kd, vb)


def _retention_tables():
    log_g = jnp.log1p(-jnp.exp2(-5.0 - jnp.arange(HEADS, dtype=F32)))
    r = jnp.arange(CHUNK, dtype=F32)
    intra = jnp.exp(jnp.abs(r[:, None] - r[None, :]) * log_g[:, None, None])
    q_dec = jnp.exp((r + 1.0) * log_g[:, None])
    k_dec = jnp.exp((CHUNK - 1.0 - r) * log_g[:, None])
    c_dec = jnp.exp(CHUNK * log_g)
    lane_head = (jnp.arange(HEADS * RET_DK) % 128) // (RET_DK // 2)
    hmask = (lane_head[None, :] == jnp.arange(HEADS)[:, None]).astype(F32)
    qdec = jnp.repeat(q_dec.T, RET_DV, axis=1)
    kdec = k_dec.T[:, lane_head]
    cdec = jnp.repeat(c_dec, RET_DV)[None, :]
    return hmask, intra, qdec, kdec, cdec


def _gelu_tanh(x):
    return 0.5 * x * (1.0 + jnp.tanh(math.sqrt(2.0 / math.pi) * (x + 0.044715 * (x * x * x))))


S5_LANE_BLOCK = 512
S5_KTILE_GROUPS = 16
S5_KTILE_IN = S5_KTILE_GROUPS * S5_CH
S5_KTILE_LANES = S5_KTILE_GROUPS * S5_STATE
S5_KTILES = S5_GROUPS // S5_KTILE_GROUPS


def _s5_kernel(p_ref, bbd_ref, cbd_ref, a_ref, d_ref, wglu_ref, o_ref, bu_ref, xs_ref, carry_ref):
    @pl.when(pl.program_id(0) == 0)
    def _():
        carry_ref[...] = jnp.zeros_like(carry_ref)

    tb = S5_TIME_BLOCK

    def gather(j):
        blk = p_ref[j, 0].reshape(SUBLANES, tb, 128)
        return jnp.swapaxes(blk, 0, 1).reshape(tb * SUBLANES, 128)

    nblk = GROUP_W // 128
    u = jnp.concatenate([gather(j) for j in range(nblk)], axis=1)
    z = jnp.concatenate([gather(nblk + j) for j in range(nblk)], axis=1)
    ub = u.astype(BF16)
    kw = 2 * S5_KTILE_LANES
    for kt in range(S5_KTILES):
        bu_ref[:, kt * kw:(kt + 1) * kw] = _dot(ub[:, kt * S5_KTILE_IN:(kt + 1) * S5_KTILE_IN],
                                                bbd_ref[kt])

    lb = S5_LANE_BLOCK
    for kt in range(S5_KTILES):
        for blk in range(S5_KTILE_LANES // lb):
            re = slice(kt * kw + blk * lb, kt * kw + (blk + 1) * lb)
            im = slice(kt * kw + S5_KTILE_LANES + blk * lb, kt * kw + S5_KTILE_LANES + (blk + 1) * lb)
            par = slice(kt * S5_KTILE_LANES + blk * lb, kt * S5_KTILE_LANES + (blk + 1) * lb)
            ar = jnp.broadcast_to(a_ref[0:1, par], (SUBLANES, lb))
            ai = jnp.broadcast_to(a_ref[1:2, par], (SUBLANES, lb))

            def two_steps(i, carry):
                xr, xi = carry
                rows = pl.ds(pl.multiple_of(i * 2 * SUBLANES, 2 * SUBLANES), 2 * SUBLANES)
                b_r = bu_ref[rows, re]
                b_i = bu_ref[rows, im]
                x1r = ar * xr - ai * xi + b_r[0:SUBLANES]
                x1i = ar * xi + ai * xr + b_i[0:SUBLANES]
                x2r = ar * x1r - ai * x1i + b_r[SUBLANES:]
                x2i = ar * x1i + ai * x1r + b_i[SUBLANES:]
                xs_ref[rows, re] = jnp.concatenate([x1r, x2r], axis=0).astype(BF16)
                xs_ref[rows, im] = jnp.concatenate([x1i, x2i], axis=0).astype(BF16)
                return x2r, x2i

            xr, xi = lax.fori_loop(0, tb // 2, two_steps,
                                   (carry_ref[0, :, par], carry_ref[1, :, par]), unroll=4)
            carry_ref[0, :, par] = xr
            carry_ref[1, :, par] = xi

    y = jnp.concatenate([_dot(xs_ref[:, kt * kw:(kt + 1) * kw], cbd_ref[kt])
                         for kt in range(S5_KTILES)], axis=1) + d_ref[...] * u
    y = _gelu_tanh(y)
    y = y * _sigmoid(_dot(y.astype(BF16), wglu_ref[...]))
    res = y * _silu(z)
    for j in range(nblk):
        blk = res[:, 128 * j:128 * (j + 1)].reshape(tb, SUBLANES, 128)
        o_ref[j, 0] = jnp.swapaxes(blk, 0, 1).reshape(SUBLANES * tb, 128)


def _s5(p_s5, w, layer, bsz, seq):
    assert bsz == SUBLANES, "the S5 kernel puts one batch element on each sublane"
    nb = seq // S5_TIME_BLOCK
    rows = bsz * S5_TIME_BLOCK
    blk = lambda j: (0, j, 0, 0)
    return pl.pallas_call(
        _s5_kernel,
        grid=(nb,),
        in_specs=[pl.BlockSpec((W_S5 // 128, 1, rows, 128), blk)] +
                 [_layer_spec(t, layer) for t in w],
        out_specs=pl.BlockSpec((GROUP_W // 128, 1, rows, 128), blk),
        out_shape=jax.ShapeDtypeStruct((GROUP_W // 128, nb, rows, 128), F32),
        scratch_shapes=[pltpu.VMEM((rows, 2 * S5_LANES), F32),
                        pltpu.VMEM((rows, 2 * S5_LANES), BF16),
                        pltpu.VMEM((2, SUBLANES, S5_LANES), F32)],
        compiler_params=_params(("arbitrary",)),
        name="s5",
    )(p_s5, *w)


def _s5_weights(a_re, a_im, log_dt, b_re, b_im, c_re, c_im, d, w_glu):
    dt = jnp.exp(log_dt)[:, None]
    mag = jnp.exp(a_re * dt)
    ab_re = mag * jnp.cos(a_im * dt)
    ab_im = mag * jnp.sin(a_im * dt)
    den = a_re * a_re + a_im * a_im
    p = ab_re - 1.0
    f_re = (p * a_re + ab_im * a_im) / den
    f_im = (ab_im * a_re - p * a_im) / den
    bb_re = f_re[..., None] * b_re - f_im[..., None] * b_im
    bb_im = f_re[..., None] * b_im + f_im[..., None] * b_re
    kg = S5_KTILE_GROUPS
    own_group = (jnp.arange(kg)[:, None, None, None] == jnp.arange(kg)[None, None, :, None])

    def tiles_in(t):
        t = jnp.swapaxes(t, 1, 2).reshape(S5_KTILES, kg, S5_CH, 1, S5_STATE)
        return jnp.where(own_group, t, 0.0).reshape(S5_KTILES, S5_KTILE_IN, S5_KTILE_LANES)

    def tiles_out(t):
        t = jnp.swapaxes(t, 1, 2).reshape(S5_KTILES, kg, S5_STATE, 1, S5_CH)
        return jnp.where(own_group, t, 0.0).reshape(S5_KTILES, S5_KTILE_LANES, S5_KTILE_IN)

    bbd = jnp.concatenate([tiles_in(bb_re), tiles_in(bb_im)], axis=2).astype(BF16)
    cbd = jnp.concatenate([tiles_out(c_re), tiles_out(-c_im)], axis=1).astype(BF16)
    a = jnp.stack([ab_re.reshape(-1), ab_im.reshape(-1)])
    return bbd, cbd, a, d[None, :], w_glu.astype(BF16)


def _log_sigmoid(x):
    return jnp.minimum(x, 0.0) - jnp.log1p(jnp.exp(-jnp.abs(x)))


MIXER_BLOCK_CHUNKS = 8


def _segmented_scan(x, row_in_chunk, combine, identity):
    sh = 1
    while sh < CHUNK:
        x = combine(x, jnp.where(row_in_chunk >= sh, pltpu.roll(x, sh, 0), identity))
        sh *= 2
    return x


def _time_on_lanes(t):
    return jnp.concatenate([t, jnp.zeros_like(t)], axis=0).T


def _ml_kernel(p_ref, g_ref, conv_ref, wq_ref, wkt_ref, bias_ref, gn_ref, o_ref,
               xbuf_ref, q_ref, kt_ref, rep_ref, cn_ref, m_ref):
    t_blk = p_ref.shape[0]
    n_chunks = t_blk // CHUNK

    @pl.when(pl.program_id(1) == 0)
    def _():
        xbuf_ref[0:SUBLANES, :] = jnp.zeros((SUBLANES, GROUP_W), F32)
        cn_ref[...] = jnp.zeros_like(cn_ref)
        m_ref[...] = jnp.zeros_like(m_ref)

    xbuf_ref[SUBLANES:, :] = p_ref[:, 0:GROUP_W]
    conv = jnp.zeros((t_blk, GROUP_W), F32)
    for j in range(ML_CONV):
        off = SUBLANES - (ML_CONV - 1) + j
        conv = conv + conv_ref[j:j + 1, :] * xbuf_ref[off:off + t_blk, :]
    xbuf_ref[0:SUBLANES, :] = xbuf_ref[t_blk:t_blk + SUBLANES, :]
    xc = _silu(conv).astype(BF16)
    for h in range(HEADS):
        cols = slice(ML_DH * h, ML_DH * (h + 1))
        q_ref[:, cols] = _dot(xc[:, cols], wq_ref[h]).astype(BF16)
        kt_ref[h] = _dot_nt(wkt_ref[h], xc[:, cols]) * ML_DH ** -0.5

    gates = g_ref[...] + bias_ref[...]
    ig = gates
    lf = _log_sigmoid(pltpu.roll(gates, 128 - HEADS, 1))
    row_in_chunk = lax.broadcasted_iota(jnp.int32, (t_blk, 128), 0) % CHUNK
    bcum = _segmented_scan(lf, row_in_chunk, jnp.add, 0.0)
    dgate = ig - bcum
    cmax = _segmented_scan(dgate, row_in_chunk, jnp.maximum, -jnp.inf)
    m_loc = bcum + cmax
    m_prev = m_ref[0:1, :]
    w_intra, w_inter, floor, wk_t, dec_scale = [], [], [], [], []
    for c in range(n_chunks):
        rows = slice(CHUNK * c, CHUNK * (c + 1))
        last = slice(CHUNK * (c + 1) - 1, CHUNK * (c + 1))
        b_last = bcum[last]
        g_max = b_last + cmax[last]
        m_new = jnp.maximum(b_last + m_prev, g_max)
        dec_scale.append(jnp.exp(b_last + m_prev - m_new))
        dec_scale.append(jnp.exp(g_max - m_new))
        inter = bcum[rows] + m_prev
        m_r = jnp.maximum(m_loc[rows], inter)
        w_inter.append(jnp.exp(inter - m_r))
        w_intra.append(jnp.exp(m_loc[rows] - m_r))
        floor.append(jnp.exp(-m_r))
        wk_t.append(jnp.exp(b_last - bcum[rows] + ig[rows] - g_max))
        m_prev = m_new
    m_ref[0:1, :] = m_prev
    pad_rows = -2 * n_chunks % SUBLANES
    if pad_rows:
        dec_scale.append(jnp.zeros((pad_rows, 128), F32))
    per_time = [cmax, jnp.concatenate(w_intra, axis=0), jnp.concatenate(w_inter, axis=0),
                jnp.concatenate(floor, axis=0), jnp.concatenate(dec_scale, axis=0)]
    offs = [0]
    for t in per_time:
        offs.append(offs[-1] + t.shape[0])
    for h in range(HEADS):
        for t, o in zip(per_time, offs):
            rep_ref[h, o:o + t.shape[0], :] = jnp.broadcast_to(t[:, h:h + 1], t.shape)

    r_i = lax.broadcasted_iota(jnp.int32, (CHUNK, CHUNK), 0)
    c_i = lax.broadcasted_iota(jnp.int32, (CHUNK, CHUNK), 1)
    causal = c_i <= r_i
    ones = jnp.ones((CHUNK, ML_DH), BF16)
    for c in range(n_chunks):
        rows = slice(CHUNK * c, CHUNK * (c + 1))
        d_t = _time_on_lanes(dgate[rows])
        wk_tt = _time_on_lanes(wk_t[c])
        for h in range(HEADS):
            cols = slice(ML_DH * h, ML_DH * (h + 1))
            rep = lambda i: rep_ref[h, offs[i] + CHUNK * c:offs[i] + CHUNK * (c + 1), :]
            cmax_b, a_b, i_b, floor_b = rep(0), rep(1), rep(2), rep(3)
            dec_b = rep_ref[h, offs[4] + 2 * c:offs[4] + 2 * c + 1, :]
            scale_b = rep_ref[h, offs[4] + 2 * c + 1:offs[4] + 2 * c + 2, :]
            qb = q_ref[rows, cols]
            kt = kt_ref[h, :, rows]
            v_ext = jnp.concatenate(
                [p_ref[rows, GROUP_W + ML_DH * h:GROUP_W + ML_DH * (h + 1)].astype(BF16), ones], axis=1)
            cn_prev = cn_ref[h]

            w = jnp.exp(jnp.where(causal, d_t[h:h + 1, 0:CHUNK] - cmax_b[:, 0:CHUNK], -jnp.inf))
            s = _dot(qb, kt.astype(BF16)) * w
            intra = _dot(s.astype(BF16), v_ext)
            inter = _dot(qb, cn_prev.astype(BF16))
            both = jnp.concatenate([a_b, a_b], axis=1) * intra + jnp.concatenate([i_b, i_b], axis=1) * inter
            hh = both[:, 0:ML_DH] / jnp.maximum(jnp.abs(both[:, ML_DH:]), floor_b)

            ktw = (kt * wk_tt[h:h + 1, 0:CHUNK]).astype(BF16)
            cn_ref[h] = (jnp.concatenate([dec_b, dec_b], axis=1) * cn_prev
                         + jnp.concatenate([scale_b, scale_b], axis=1) * _dot(ktw, v_ext))

            og = p_ref[rows, 2 * GROUP_W + ML_DH * h:2 * GROUP_W + ML_DH * (h + 1)]
            z = p_ref[rows, 3 * GROUP_W + ML_DH * h:3 * GROUP_W + ML_DH * (h + 1)]
            o_ref[rows, cols] = (_norm_rows(hh * _sigmoid(og), GN_EPS) * gn_ref[:, cols] * _silu(z))


def _xa_kernel(p_ref, mk_ref, mv_ref, o_ref):
    z = p_ref[:, GROUP_W:]
    outs = []
    for h in range(HEADS):
        cols = slice(XA_DH * h, XA_DH * (h + 1))
        s = _dot_nt(p_ref[:, cols].astype(BF16), mk_ref[:, cols]) * XA_DH ** -0.5
        e = jnp.exp(s - jnp.max(s, axis=1, keepdims=True))
        outs.append(_dot(e.astype(BF16), mv_ref[:, cols]) / jnp.sum(e, axis=1, keepdims=True))
    o_ref[...] = jnp.concatenate(outs, axis=1) * _silu(z)


N_RET_IN, N_ML_IN, N_XA_IN = 8, 7, 3


def _mixers_kernel(*refs):
    i0 = 0
    ret_in = refs[i0:i0 + N_RET_IN]; i0 += N_RET_IN
    ml_in = refs[i0:i0 + N_ML_IN]; i0 += N_ML_IN
    xa_in = refs[i0:i0 + N_XA_IN]; i0 += N_XA_IN
    o_ret, o_ml, o_xa = refs[i0:i0 + 3]; i0 += 3
    ret_scratch = refs[i0:i0 + 1]; i0 += 1
    ml_scratch = refs[i0:]
    _xa_kernel(*xa_in, o_xa)
    _ret_kernel(*ret_in, o_ret, *ret_scratch)
    _ml_kernel(*ml_in, o_ml, *ml_scratch)


def _mixers(p_ret, cos, sin, tabs, p_ml, p_if, ml_w, layer, p_xa, mk, mv, bsz, seq):
    t_blk = min(MIXER_BLOCK_CHUNKS * CHUNK, seq)
    nb = seq // t_blk
    n_chunks = t_blk // CHUNK
    rep_rows = 4 * t_blk + 2 * n_chunks + (-2 * n_chunks % SUBLANES)
    row = lambda b, c: (b * nb + c, 0)
    mem = lambda b, c: (b, 0)
    const = lambda t: pl.BlockSpec(t.shape, lambda b, c, nd=t.ndim: (0,) * nd)
    out = jax.ShapeDtypeStruct((bsz * seq, GROUP_W), F32)
    return pl.pallas_call(
        _mixers_kernel,
        grid=(bsz, nb),
        in_specs=[pl.BlockSpec((t_blk, W_RET), row),
                  pl.BlockSpec((t_blk, 128), row),
                  pl.BlockSpec((t_blk, 128), row)] + [const(t) for t in tabs] +
                 [pl.BlockSpec((t_blk, W_ML), row),
                  pl.BlockSpec((t_blk, W_IF), row)] + [_layer_spec(t, layer) for t in ml_w] +
                 [pl.BlockSpec((t_blk, W_XA), row),
                  pl.BlockSpec((N_MEM, GROUP_W), mem),
                  pl.BlockSpec((N_MEM, GROUP_W), mem)],
        out_specs=[pl.BlockSpec((t_blk, GROUP_W), row)] * 3,
        out_shape=[out] * 3,
        scratch_shapes=[pltpu.VMEM((HEADS * RET_DK, HEADS * RET_DV), F32),
                        pltpu.VMEM((t_blk + SUBLANES, GROUP_W), F32),
                        pltpu.VMEM((t_blk, GROUP_W), BF16),
                        pltpu.VMEM((HEADS, ML_DH, t_blk), F32),
                        pltpu.VMEM((HEADS, rep_rows, 128), F32),
                        pltpu.VMEM((HEADS, ML_DH, 2 * ML_DH), F32),
                        pltpu.VMEM((SUBLANES, 128), F32)],
        compiler_params=_params(("parallel", "arbitrary")),
        name="mixers",
    )(p_ret, cos, sin, *tabs, p_ml, p_if, *ml_w, p_xa, mk, mv)


def _out_kernel(x_ref, r_ref, s_ref, m_ref, a_ref, w_ref, g_ref, b_ref, o_ref):
    y = None
    s5o = jnp.concatenate(
        [jnp.concatenate([s_ref[j, t] for t in range(s_ref.shape[1])], axis=0)
         for j in range(s_ref.shape[0])], axis=1)
    for i, t in enumerate((r_ref[...], s5o, m_ref[...], a_ref[...])):
        part = _dot(t.astype(BF16), w_ref[GROUP_W * i:GROUP_W * (i + 1), :])
        y = part if y is None else y + part
    t = ALPHA * x_ref[...] + y
    o_ref[...] = _norm_rows(t, LN_EPS) * g_ref[...] + b_ref[...]


def _out_project(x2, groups, w_out, ln_g, ln_b, layer, seq):
    m = x2.shape[0]
    tm = min(PROJ_ROWS, seq)
    nj = seq // tm
    row = lambda i: (i, 0)
    return pl.pallas_call(
        _out_kernel,
        grid=(m // tm,),
        in_specs=[pl.BlockSpec((tm, D_MODEL), row),
                  pl.BlockSpec((tm, GROUP_W), row),
                  pl.BlockSpec((GROUP_W // 128, tm // S5_TIME_BLOCK, S5_TIME_BLOCK, 128),
                               lambda i: (0, i % nj, i // nj, 0)),
                  pl.BlockSpec((tm, GROUP_W), row),
                  pl.BlockSpec((tm, GROUP_W), row),
                  _layer_spec(w_out, layer),
                  _layer_spec(ln_g, layer),
                  _layer_spec(ln_b, layer)],
        out_specs=pl.BlockSpec((tm, D_MODEL), row),
        out_shape=jax.ShapeDtypeStruct((m, D_MODEL), F32),
        compiler_params=_params(("parallel",)),
        name="out_proj_norm",
    )(x2, *groups, w_out, ln_g, ln_b)


def _split_w_in(w_in):
    half = RET_DK // 2
    perm = jnp.array([h * RET_DK + s * half + j for s in range(2) for h in range(HEADS)
                      for j in range(half)], jnp.int32)
    n_qk = HEADS * RET_DK
    w_qk = jnp.concatenate([w_in[:, :n_qk][:, perm],
                            w_in[:, n_qk:2 * n_qk][:, perm] * RET_DK ** -0.5],
                           axis=1).astype(BF16)
    w_xa = w_in[:, C_AQ:C_AQ + W_XA].astype(BF16)
    w_if = jnp.pad(w_in[:, C_MI:C_AQ], ((0, 0), (0, W_IF - 2 * HEADS))).astype(BF16)
    return w_in[:, :C_MI].astype(BF16), w_qk, w_xa, w_if


def _prepare_weights(w_in, s5_a_re, s5_a_im, s5_log_dt, s5_b_re, s5_b_im, s5_c_re, s5_c_im,
                     s5_d, s5_w_glu, ml_conv_w, ml_wq, ml_wk, ml_b_ig, ml_b_fg, ml_gn, xa_w_kv, w_out,
                     ln_g, ln_b):
    bias = jnp.concatenate([ml_b_ig, ml_b_fg, jnp.zeros((W_IF - 2 * HEADS,), F32)])[None, :]
    return dict(
        w_in=_split_w_in(w_in),
        s5=_s5_weights(s5_a_re, s5_a_im, s5_log_dt, s5_b_re, s5_b_im, s5_c_re, s5_c_im, s5_d,
                       s5_w_glu),
        ml=(ml_conv_w, ml_wq.astype(BF16), jnp.swapaxes(ml_wk, 1, 2).astype(BF16), bias,
            ml_gn[None, :]),
        w_kv=xa_w_kv.astype(BF16), w_out=w_out.astype(BF16), ln_g=ln_g[None, :], ln_b=ln_b[None, :])


def _layer(x2, mem2, cos, sin, ret_tabs, bsz, seq, w, layer):
    p_ml, p_s5, p_xa, p_ret, p_if = _project(x2, w["w_in"], layer, bsz, seq)
    s5o = _s5(p_s5, w["s5"], layer, bsz, seq)
    mk, mv = _memory_kv(mem2, w["w_kv"], layer)
    ret, ml, xa = _mixers(p_ret, cos, sin, ret_tabs, p_ml, p_if, w["ml"], layer, p_xa, mk, mv,
                          bsz, seq)
    return _out_project(x2, (ret, s5o, ml, xa), w["w_out"], w["ln_g"], w["ln_b"], layer, seq)


def kernel(x, mem, positions, w_in, s5_a_re, s5_a_im, s5_log_dt, s5_b_re, s5_b_im, s5_c_re, s5_c_im,
           s5_d, s5_w_glu, ml_conv_w, ml_wq, ml_wk, ml_b_ig, ml_b_fg, ml_gn, xa_w_kv, w_out, ln_g,
           ln_b):
    bsz, seq, _ = x.shape
    x2 = x.reshape(bsz * seq, D_MODEL)
    mem2 = mem.reshape(bsz * N_MEM, D_MODEL)
    cos, sin = _rope_tables(positions)
    ret_tabs = _retention_tables()
    weights = jax.vmap(_prepare_weights)(
        w_in, s5_a_re, s5_a_im, s5_log_dt, s5_b_re, s5_b_im, s5_c_re, s5_c_im, s5_d, s5_w_glu,
        ml_conv_w, ml_wq, ml_wk, ml_b_ig, ml_b_fg, ml_gn, xa_w_kv, w_out, ln_g, ln_b)
    for l in range(w_in.shape[0]):
        x2 = _layer(x2, mem2, cos, sin, ret_tabs, bsz, seq, weights, l)
    return x2.reshape(bsz, seq, D_MODEL)
```

```python
import functools
import math

import jax
import jax.numpy as jnp
from jax import lax
from jax.experimental import pallas as pl
from jax.experimental.pallas import tpu as pltpu

F32 = jnp.float32
BF16 = jnp.bfloat16

D_MODEL = 1024
CHUNK = 64
N_MEM = 256
GROUP_W = 512
HEADS = 4
RET_DK = 64
RET_DV = 128
ROPE_BASE = 10000.0
S5_CH = 16
S5_GROUPS = 32
S5_STATE = 64
S5_LANES = S5_GROUPS * S5_STATE
ML_DH = 128
ML_CONV = 4
XA_DH = 128
LN_EPS = 1e-5
GN_EPS = 1e-6
DEPTH = 4
ALPHA = (2 * DEPTH) ** 0.25

SUBLANES = 8
VMEM_LIMIT = 56 * 1024 * 1024

W_ML, W_S5, W_XA, W_RET, W_IF = 4 * GROUP_W, 2 * GROUP_W, 2 * GROUP_W, 3 * GROUP_W, 128


def _params(sem):
    return pltpu.CompilerParams(dimension_semantics=sem, vmem_limit_bytes=VMEM_LIMIT)


def _sigmoid(x):
    return 1.0 / (1.0 + jnp.exp(-x))


def _silu(x):
    return x * _sigmoid(x)


def _norm_rows(t, eps):
    mu = jnp.mean(t, axis=-1, keepdims=True)
    d = t - mu
    var = jnp.mean(d * d, axis=-1, keepdims=True)
    return d * lax.rsqrt(var + eps)


def _layer_spec(t, layer, **kw):
    return pl.BlockSpec((None,) + t.shape[1:], lambda *_: (layer,) + (0,) * (t.ndim - 1), **kw)


def _dot(a, b):
    return jnp.dot(a, b, preferred_element_type=F32)


def _dot_nt(a, b):
    return lax.dot_general(a, b, (((1,), (1,)), ((), ())), preferred_element_type=F32)


def _dot_tn(a, b):
    return lax.dot_general(a, b, (((0,), (0,)), ((), ())), preferred_element_type=F32)


def _rope_kernel(pos_ref, inv_ref, cos_ref, sin_ref):
    ang = pos_ref[...].astype(F32) * inv_ref[...]
    cos_ref[...] = jnp.cos(ang)
    sin_ref[...] = jnp.sin(ang)


def _rope_tables(positions):
    m = positions.size
    half = RET_DK // 2
    inv = ROPE_BASE ** (-jnp.arange(half, dtype=F32) / half)
    inv = jnp.tile(inv, HEADS)[None, :]
    tm = min(512, m)
    return pl.pallas_call(
        _rope_kernel,
        grid=(m // tm,),
        in_specs=[pl.BlockSpec((tm, 1), lambda i: (i, 0)),
                  pl.BlockSpec((1, 128), lambda i: (0, 0))],
        out_specs=[pl.BlockSpec((tm, 128), lambda i: (i, 0))] * 2,
        out_shape=[jax.ShapeDtypeStruct((m, 128), F32)] * 2,
        compiler_params=_params(("parallel",)),
        name="rope_tables",
    )(positions.reshape(m, 1), inv)


PROJ_ROWS = 512
S5_TIME_BLOCK = 2 * CHUNK

C_RV, C_SU, C_MX, C_MI, C_AQ = 512, 1536, 2560, 4608, 4616


def _proj_kernel(x_ref, w_ref, wqk_ref, wxa_ref, wif_ref, o_ml, o_s5, o_xa, o_ret, o_if):
    xb = x_ref[...].astype(BF16)
    o_ml[...] = _dot(xb, w_ref[:, C_MX:C_MX + W_ML])
    tb = S5_TIME_BLOCK
    for j in range(W_S5 // 256):
        r = _dot(xb, w_ref[:, C_SU + 256 * j:C_SU + 256 * (j + 1)])
        for half in range(2):
            for t in range(o_s5.shape[1]):
                o_s5[2 * j + half, t] = r[tb * t:tb * (t + 1), 128 * half:128 * (half + 1)]
    o_xa[...] = _dot(xb, wxa_ref[...])
    o_ret[:, 0:2 * HEADS * RET_DK] = _dot(xb, wqk_ref[...])
    o_ret[:, 2 * HEADS * RET_DK:] = _dot(xb, w_ref[:, C_RV:C_SU])
    o_if[...] = _dot(xb, wif_ref[...])


def _project(x2, w, layer, bsz, seq):
    w_main, w_qk, w_xa, w_if = w
    m = x2.shape[0]
    tm = min(PROJ_ROWS, seq)
    nj = seq // tm
    tpb = tm // S5_TIME_BLOCK
    row = lambda i: (i, 0)
    s5_shape = (W_S5 // 128, seq // S5_TIME_BLOCK, bsz * S5_TIME_BLOCK, 128)
    specs = [pl.BlockSpec((tm, W_ML), row),
             pl.BlockSpec((W_S5 // 128, tpb, S5_TIME_BLOCK, 128),
                          lambda i: (0, i % nj, i // nj, 0)),
             pl.BlockSpec((tm, W_XA), row),
             pl.BlockSpec((tm, W_RET), row),
             pl.BlockSpec((tm, W_IF), row)]
    shapes = [(m, W_ML), s5_shape, (m, W_XA), (m, W_RET), (m, W_IF)]
    return pl.pallas_call(
        _proj_kernel,
        grid=(m // tm,),
        in_specs=[pl.BlockSpec((tm, D_MODEL), row)] +
                 [_layer_spec(t, layer, pipeline_mode=pl.Buffered(1))
                  for t in (w_main, w_qk, w_xa, w_if)],
        out_specs=specs,
        out_shape=[jax.ShapeDtypeStruct(sh, F32) for sh in shapes],
        compiler_params=_params(("parallel",)),
        name="in_proj",
    )(x2, w_main, w_qk, w_xa, w_if)


def _kv_kernel(m_ref, w_ref, k_ref, v_ref):
    mb = m_ref[...].astype(BF16)
    k_ref[...] = _dot(mb, w_ref[:, :GROUP_W]).astype(BF16)
    v_ref[...] = _dot(mb, w_ref[:, GROUP_W:]).astype(BF16)


def _memory_kv(mem2, w_kv, layer):
    m = mem2.shape[0]
    return pl.pallas_call(
        _kv_kernel,
        grid=(m // N_MEM,),
        in_specs=[pl.BlockSpec((N_MEM, D_MODEL), lambda i: (i, 0)),
                  _layer_spec(w_kv, layer)],
        out_specs=[pl.BlockSpec((N_MEM, GROUP_W), lambda i: (i, 0))] * 2,
        out_shape=[jax.ShapeDtypeStruct((m, GROUP_W), BF16)] * 2,
        compiler_params=_params(("parallel",)),
        name="memory_kv",
    )(mem2, w_kv)


def _ret_kernel(p_ref, cos_ref, sin_ref, hmask_ref, intra_ref, qdec_ref, kdec_ref, cdec_ref,
                o_ref, s_ref):
    @pl.when(pl.program_id(1) == 0)
    def _():
        s_ref[...] = jnp.zeros_like(s_ref)

    cs = cos_ref[...]
    sn = sin_ref[...]

    def rot(t):
        t1 = t[:, :128]
        t2 = t[:, 128:]
        return jnp.concatenate([t1 * cs - t2 * sn, t1 * sn + t2 * cs], axis=1)

    qr = rot(p_ref[:, 0:256])
    kr = rot(p_ref[:, 256:512])
    for c in range(p_ref.shape[0] // CHUNK):
        rows = slice(CHUNK * c, CHUNK * (c + 1))
        vb = p_ref[rows, 512:1024].astype(BF16)
        krb = kr[rows].astype(BF16)
        state = s_ref[...]
        sb = state.astype(BF16)
        for h in range(HEADS):
            cols = slice(RET_DV * h, RET_DV * (h + 1))
            qm = (qr[rows] * hmask_ref[h:h + 1, :]).astype(BF16)
            att = (_dot_nt(qm, krb) * intra_ref[h]).astype(BF16)
            o = _dot(att, vb[:, cols]) + qdec_ref[:, cols] * _dot(qm, sb[:, cols])
            z = p_ref[rows, 1024 + RET_DV * h:1024 + RET_DV * (h + 1)]
            o_ref[rows, cols] = _norm_rows(o, GN_EPS) * _silu(z)
        kd = (kr[rows] * kdec_ref[...]).astype(BF16)
        s_ref[...] = cdec_ref[...] * state + _dot_tn(kd, vb)


def _retention_tables():
    log_g = jnp.log1p(-jnp.exp2(-5.0 - jnp.arange(HEADS, dtype=F32)))
    r = jnp.arange(CHUNK, dtype=F32)
    intra = jnp.exp(jnp.abs(r[:, None] - r[None, :]) * log_g[:, None, None])
    q_dec = jnp.exp((r + 1.0) * log_g[:, None])
    k_dec = jnp.exp((CHUNK - 1.0 - r) * log_g[:, None])
    c_dec = jnp.exp(CHUNK * log_g)
    lane_head = (jnp.arange(HEADS * RET_DK) % 128) // (RET_DK // 2)
    hmask = (lane_head[None, :] == jnp.arange(HEADS)[:, None]).astype(F32)
    qdec = jnp.repeat(q_dec.T, RET_DV, axis=1)
    kdec = k_dec.T[:, lane_head]
    cdec = jnp.repeat(c_dec, RET_DV)[None, :]
    return hmask, intra, qdec, kdec, cdec


def _gelu_tanh(x):
    return 0.5 * x * (1.0 + jnp.tanh(math.sqrt(2.0 / math.pi) * (x + 0.044715 * (x * x * x))))


S5_LANE_BLOCK = 512
S5_KTILE_GROUPS = 16
S5_KTILE_IN = S5_KTILE_GROUPS * S5_CH
S5_KTILE_LANES = S5_KTILE_GROUPS * S5_STATE
S5_KTILES = S5_GROUPS // S5_KTILE_GROUPS


def _s5_kernel(p_ref, bbd_ref, cbd_ref, a_ref, d_ref, wglu_ref, o_ref, bu_ref, xs_ref, carry_ref):
    @pl.when(pl.program_id(0) == 0)
    def _():
        carry_ref[...] = jnp.zeros_like(carry_ref)

    tb = S5_TIME_BLOCK

    def gather(j):
        blk = p_ref[j, 0].reshape(SUBLANES, tb, 128)
        return jnp.swapaxes(blk, 0, 1).reshape(tb * SUBLANES, 128)

    nblk = GROUP_W // 128
    u = jnp.concatenate([gather(j) for j in range(nblk)], axis=1)
    z = jnp.concatenate([gather(nblk + j) for j in range(nblk)], axis=1)
    ub = u.astype(BF16)
    kw = 2 * S5_KTILE_LANES
    for kt in range(S5_KTILES):
        bu_ref[:, kt * kw:(kt + 1) * kw] = _dot(ub[:, kt * S5_KTILE_IN:(kt + 1) * S5_KTILE_IN],
                                                bbd_ref[kt])

    lb = S5_LANE_BLOCK
    for kt in range(S5_KTILES):
        for blk in range(S5_KTILE_LANES // lb):
            re = slice(kt * kw + blk * lb, kt * kw + (blk + 1) * lb)
            im = slice(kt * kw + S5_KTILE_LANES + blk * lb, kt * kw + S5_KTILE_LANES + (blk + 1) * lb)
            par = slice(kt * S5_KTILE_LANES + blk * lb, kt * S5_KTILE_LANES + (blk + 1) * lb)
            ar = jnp.broadcast_to(a_ref[0:1, par], (SUBLANES, lb))
            ai = jnp.broadcast_to(a_ref[1:2, par], (SUBLANES, lb))

            def two_steps(i, carry):
                xr, xi = carry
                rows = pl.ds(pl.multiple_of(i * 2 * SUBLANES, 2 * SUBLANES), 2 * SUBLANES)
                b_r = bu_ref[rows, re]
                b_i = bu_ref[rows, im]
                x1r = ar * xr - ai * xi + b_r[0:SUBLANES]
                x1i = ar * xi + ai * xr + b_i[0:SUBLANES]
                x2r = ar * x1r - ai * x1i + b_r[SUBLANES:]
                x2i = ar * x1i + ai * x1r + b_i[SUBLANES:]
                xs_ref[rows, re] = jnp.concatenate([x1r, x2r], axis=0).astype(BF16)
                xs_ref[rows, im] = jnp.concatenate([x1i, x2i], axis=0).astype(BF16)
                return x2r, x2i

            xr, xi = lax.fori_loop(0, tb // 2, two_steps,
                                   (carry_ref[0, :, par], carry_ref[1, :, par]), unroll=4)
            carry_ref[0, :, par] = xr
            carry_ref[1, :, par] = xi

    y = jnp.concatenate([_dot(xs_ref[:, kt * kw:(kt + 1) * kw], cbd_ref[kt])
                         for kt in range(S5_KTILES)], axis=1) + d_ref[...] * u
    y = _gelu_tanh(y)
    y = y * _sigmoid(_dot(y.astype(BF16), wglu_ref[...]))
    res = y * _silu(z)
    for j in range(nblk):
        blk = res[:, 128 * j:128 * (j + 1)].reshape(tb, SUBLANES, 128)
        o_ref[j, 0] = jnp.swapaxes(blk, 0, 1).reshape(SUBLANES * tb, 128)


def _s5(p_s5, w, layer, bsz, seq):
    assert bsz == SUBLANES, "the S5 kernel puts one batch element on each sublane"
    nb = seq // S5_TIME_BLOCK
    rows = bsz * S5_TIME_BLOCK
    blk = lambda j: (0, j, 0, 0)
    return pl.pallas_call(
        _s5_kernel,
        grid=(nb,),
        in_specs=[pl.BlockSpec((W_S5 // 128, 1, rows, 128), blk)] +
                 [_layer_spec(t, layer) for t in w],
        out_specs=pl.BlockSpec((GROUP_W // 128, 1, rows, 128), blk),
        out_shape=jax.ShapeDtypeStruct((GROUP_W // 128, nb, rows, 128), F32),
        scratch_shapes=[pltpu.VMEM((rows, 2 * S5_LANES), F32),
                        pltpu.VMEM((rows, 2 * S5_LANES), BF16),
                        pltpu.VMEM((2, SUBLANES, S5_LANES), F32)],
        compiler_params=_params(("arbitrary",)),
        name="s5",
    )(p_s5, *w)


def _s5_weights(a_re, a_im, log_dt, b_re, b_im, c_re, c_im, d, w_glu):
    dt = jnp.exp(log_dt)[:, None]
    mag = jnp.exp(a_re * dt)
    ab_re = mag * jnp.cos(a_im * dt)
    ab_im = mag * jnp.sin(a_im * dt)
    den = a_re * a_re + a_im * a_im
    p = ab_re - 1.0
    f_re = (p * a_re + ab_im * a_im) / den
    f_im = (ab_im * a_re - p * a_im) / den
    bb_re = f_re[..., None] * b_re - f_im[..., None] * b_im
    bb_im = f_re[..., None] * b_im + f_im[..., None] * b_re
    kg = S5_KTILE_GROUPS
    own_group = (jnp.arange(kg)[:, None, None, None] == jnp.arange(kg)[None, None, :, None])

    def tiles_in(t):
        t = jnp.swapaxes(t, 1, 2).reshape(S5_KTILES, kg, S5_CH, 1, S5_STATE)
        return jnp.where(own_group, t, 0.0).reshape(S5_KTILES, S5_KTILE_IN, S5_KTILE_LANES)

    def tiles_out(t):
        t = jnp.swapaxes(t, 1, 2).reshape(S5_KTILES, kg, S5_STATE, 1, S5_CH)
        return jnp.where(own_group, t, 0.0).reshape(S5_KTILES, S5_KTILE_LANES, S5_KTILE_IN)

    bbd = jnp.concatenate([tiles_in(bb_re), tiles_in(bb_im)], axis=2).astype(BF16)
    cbd = jnp.concatenate([tiles_out(c_re), tiles_out(-c_im)], axis=1).astype(BF16)
    a = jnp.stack([ab_re.reshape(-1), ab_im.reshape(-1)])
    return bbd, cbd, a, d[None, :], w_glu.astype(BF16)


def _log_sigmoid(x):
    return jnp.minimum(x, 0.0) - jnp.log1p(jnp.exp(-jnp.abs(x)))


MIXER_BLOCK_CHUNKS = 8


def _segmented_scan(x, row_in_chunk, combine, identity):
    sh = 1
    while sh < CHUNK:
        x = combine(x, jnp.where(row_in_chunk >= sh, pltpu.roll(x, sh, 0), identity))
        sh *= 2
    return x


def _time_on_lanes(t):
    return jnp.concatenate([t, jnp.zeros_like(t)], axis=0).T


def _ml_kernel(p_ref, g_ref, conv_ref, wq_ref, wkt_ref, bias_ref, gn_ref, o_ref,
               xbuf_ref, q_ref, kt_ref, rep_ref, cn_ref, m_ref):
    t_blk = p_ref.shape[0]
    n_chunks = t_blk // CHUNK

    @pl.when(pl.program_id(1) == 0)
    def _():
        xbuf_ref[0:SUBLANES, :] = jnp.zeros((SUBLANES, GROUP_W), F32)
        cn_ref[...] = jnp.zeros_like(cn_ref)
        m_ref[...] = jnp.zeros_like(m_ref)

    xbuf_ref[SUBLANES:, :] = p_ref[:, 0:GROUP_W]
    conv = jnp.zeros((t_blk, GROUP_W), F32)
    for j in range(ML_CONV):
        off = SUBLANES - (ML_CONV - 1) + j
        conv = conv + conv_ref[j:j + 1, :] * xbuf_ref[off:off + t_blk, :]
    xbuf_ref[0:SUBLANES, :] = xbuf_ref[t_blk:t_blk + SUBLANES, :]
    xc = _silu(conv).astype(BF16)
    for h in range(HEADS):
        cols = slice(ML_DH * h, ML_DH * (h + 1))
        q_ref[:, cols] = _dot(xc[:, cols], wq_ref[h]).astype(BF16)
        kt_ref[h] = _dot_nt(wkt_ref[h], xc[:, cols]) * ML_DH ** -0.5

    gates = g_ref[...] + bias_ref[...]
    ig = gates
    lf = _log_sigmoid(pltpu.roll(gates, 128 - HEADS, 1))
    row_in_chunk = lax.broadcasted_iota(jnp.int32, (t_blk, 128), 0) % CHUNK
    bcum = _segmented_scan(lf, row_in_chunk, jnp.add, 0.0)
    dgate = ig - bcum
    cmax = _segmented_scan(dgate, row_in_chunk, jnp.maximum, -jnp.inf)
    m_loc = bcum + cmax
    m_prev = m_ref[0:1, :]
    w_intra, w_inter, floor, wk_t, dec_scale = [], [], [], [], []
    for c in range(n_chunks):
        rows = slice(CHUNK * c, CHUNK * (c + 1))
        last = slice(CHUNK * (c + 1) - 1, CHUNK * (c + 1))
        b_last = bcum[last]
        g_max = b_last + cmax[last]
        m_new = jnp.maximum(b_last + m_prev, g_max)
        dec_scale.append(jnp.exp(b_last + m_prev - m_new))
        dec_scale.append(jnp.exp(g_max - m_new))
        inter = bcum[rows] + m_prev
        m_r = jnp.maximum(m_loc[rows], inter)
        w_inter.append(jnp.exp(inter - m_r))
        w_intra.append(jnp.exp(m_loc[rows] - m_r))
        floor.append(jnp.exp(-m_r))
        wk_t.append(jnp.exp(b_last - bcum[rows] + ig[rows] - g_max))
        m_prev = m_new
    m_ref[0:1, :] = m_prev
    pad_rows = -2 * n_chunks % SUBLANES
    if pad_rows:
        dec_scale.append(jnp.zeros((pad_rows, 128), F32))
    per_time = [cmax, jnp.concatenate(w_intra, axis=0), jnp.concatenate(w_inter, axis=0),
                jnp.concatenate(floor, axis=0), jnp.concatenate(dec_scale, axis=0)]
    offs = [0]
    for t in per_time:
        offs.append(offs[-1] + t.shape[0])
    for h in range(HEADS):
        for t, o in zip(per_time, offs):
            rep_ref[h, o:o + t.shape[0], :] = jnp.broadcast_to(t[:, h:h + 1], t.shape)

    r_i = lax.broadcasted_iota(jnp.int32, (CHUNK, CHUNK), 0)
    c_i = lax.broadcasted_iota(jnp.int32, (CHUNK, CHUNK), 1)
    causal = c_i <= r_i
    ones = jnp.ones((CHUNK, ML_DH), BF16)
    for c in range(n_chunks):
        rows = slice(CHUNK * c, CHUNK * (c + 1))
        d_t = _time_on_lanes(dgate[rows])
        wk_tt = _time_on_lanes(wk_t[c])
        for h in range(HEADS):
            cols = slice(ML_DH * h, ML_DH * (h + 1))
            rep = lambda i: rep_ref[h, offs[i] + CHUNK * c:offs[i] + CHUNK * (c + 1), :]
            cmax_b, a_b, i_b, floor_b = rep(0), rep(1), rep(2), rep(3)
            dec_b = rep_ref[h, offs[4] + 2 * c:offs[4] + 2 * c + 1, :]
            scale_b = rep_ref[h, offs[4] + 2 * c + 1:offs[4] + 2 * c + 2, :]
            qb = q_ref[rows, cols]
            kt = kt_ref[h, :, rows]
            v_ext = jnp.concatenate(
                [p_ref[rows, GROUP_W + ML_DH * h:GROUP_W + ML_DH * (h + 1)].astype(BF16), ones], axis=1)
            cn_prev = cn_ref[h]

            w = jnp.exp(jnp.where(causal, d_t[h:h + 1, 0:CHUNK] - cmax_b[:, 0:CHUNK], -jnp.inf))
            s = _dot(qb, kt.astype(BF16)) * w
            intra = _dot(s.astype(BF16), v_ext)
            inter = _dot(qb, cn_prev.astype(BF16))
            both = jnp.concatenate([a_b, a_b], axis=1) * intra + jnp.concatenate([i_b, i_b], axis=1) * inter
            hh = both[:, 0:ML_DH] / jnp.maximum(jnp.abs(both[:, ML_DH:]), floor_b)

            ktw = (kt * wk_tt[h:h + 1, 0:CHUNK]).astype(BF16)
            cn_ref[h] = (jnp.concatenate([dec_b, dec_b], axis=1) * cn_prev
                         + jnp.concatenate([scale_b, scale_b], axis=1) * _dot(ktw, v_ext))

            og = p_ref[rows, 2 * GROUP_W + ML_DH * h:2 * GROUP_W + ML_DH * (h + 1)]
            z = p_ref[rows, 3 * GROUP_W + ML_DH * h:3 * GROUP_W + ML_DH * (h + 1)]
            o_ref[rows, cols] = (_norm_rows(hh * _sigmoid(og), GN_EPS) * gn_ref[:, cols] * _silu(z))


def _xa_kernel(p_ref, mk_ref, mv_ref, o_ref):
    z = p_ref[:, GROUP_W:]
    outs = []
    for h in range(HEADS):
        cols = slice(XA_DH * h, XA_DH * (h + 1))
        s = _dot_nt(p_ref[:, cols].astype(BF16), mk_ref[:, cols]) * XA_DH ** -0.5
        e = jnp.exp(s - jnp.max(s, axis=1, keepdims=True))
        outs.append(_dot(e.astype(BF16), mv_ref[:, cols]) / jnp.sum(e, axis=1, keepdims=True))
    o_ref[...] = jnp.concatenate(outs, axis=1) * _silu(z)


N_RET_IN, N_ML_IN, N_XA_IN = 8, 7, 3


def _mixers_kernel(*refs):
    i0 = 0
    ret_in = refs[i0:i0 + N_RET_IN]; i0 += N_RET_IN
    ml_in = refs[i0:i0 + N_ML_IN]; i0 += N_ML_IN
    xa_in = refs[i0:i0 + N_XA_IN]; i0 += N_XA_IN
    o_ret, o_ml, o_xa = refs[i0:i0 + 3]; i0 += 3
    ret_scratch = refs[i0:i0 + 1]; i0 += 1
    ml_scratch = refs[i0:]
    _xa_kernel(*xa_in, o_xa)
    _ret_kernel(*ret_in, o_ret, *ret_scratch)
    _ml_kernel(*ml_in, o_ml, *ml_scratch)


def _mixers(p_ret, cos, sin, tabs, p_ml, p_if, ml_w, layer, p_xa, mk, mv, bsz, seq):
    t_blk = min(MIXER_BLOCK_CHUNKS * CHUNK, seq)
    nb = seq // t_blk
    n_chunks = t_blk // CHUNK
    rep_rows = 4 * t_blk + 2 * n_chunks + (-2 * n_chunks % SUBLANES)
    row = lambda b, c: (b * nb + c, 0)
    mem = lambda b, c: (b, 0)
    const = lambda t: pl.BlockSpec(t.shape, lambda b, c, nd=t.ndim: (0,) * nd)
    out = jax.ShapeDtypeStruct((bsz * seq, GROUP_W), F32)
    return pl.pallas_call(
        _mixers_kernel,
        grid=(bsz, nb),
        in_specs=[pl.BlockSpec((t_blk, W_RET), row),
                  pl.BlockSpec((t_blk, 128), row),
                  pl.BlockSpec((t_blk, 128), row)] + [const(t) for t in tabs] +
                 [pl.BlockSpec((t_blk, W_ML), row),
                  pl.BlockSpec((t_blk, W_IF), row)] + [_layer_spec(t, layer) for t in ml_w] +
                 [pl.BlockSpec((t_blk, W_XA), row),
                  pl.BlockSpec((N_MEM, GROUP_W), mem),
                  pl.BlockSpec((N_MEM, GROUP_W), mem)],
        out_specs=[pl.BlockSpec((t_blk, GROUP_W), row)] * 3,
        out_shape=[out] * 3,
        scratch_shapes=[pltpu.VMEM((HEADS * RET_DK, HEADS * RET_DV), F32),
                        pltpu.VMEM((t_blk + SUBLANES, GROUP_W), F32),
                        pltpu.VMEM((t_blk, GROUP_W), BF16),
                        pltpu.VMEM((HEADS, ML_DH, t_blk), F32),
                        pltpu.VMEM((HEADS, rep_rows, 128), F32),
                        pltpu.VMEM((HEADS, ML_DH, 2 * ML_DH), F32),
                        pltpu.VMEM((SUBLANES, 128), F32)],
        compiler_params=_params(("parallel", "arbitrary")),
        name="mixers",
    )(p_ret, cos, sin, *tabs, p_ml, p_if, *ml_w, p_xa, mk, mv)


def _out_kernel(x_ref, r_ref, s_ref, m_ref, a_ref, w_ref, g_ref, b_ref, o_ref):
    y = None
    s5o = jnp.concatenate(
        [jnp.concatenate([s_ref[j, t] for t in range(s_ref.shape[1])], axis=0)
         for j in range(s_ref.shape[0])], axis=1)
    for i, t in enumerate((r_ref[...], s5o, m_ref[...], a_ref[...])):
        part = _dot(t.astype(BF16), w_ref[GROUP_W * i:GROUP_W * (i + 1), :])
        y = part if y is None else y + part
    t = ALPHA * x_ref[...] + y
    o_ref[...] = _norm_rows(t, LN_EPS) * g_ref[...] + b_ref[...]


def _out_project(x2, groups, w_out, ln_g, ln_b, layer, seq):
    m = x2.shape[0]
    tm = min(PROJ_ROWS, seq)
    nj = seq // tm
    row = lambda i: (i, 0)
    return pl.pallas_call(
        _out_kernel,
        grid=(m // tm,),
        in_specs=[pl.BlockSpec((tm, D_MODEL), row),
                  pl.BlockSpec((tm, GROUP_W), row),
                  pl.BlockSpec((GROUP_W // 128, tm // S5_TIME_BLOCK, S5_TIME_BLOCK, 128),
                               lambda i: (0, i % nj, i // nj, 0)),
                  pl.BlockSpec((tm, GROUP_W), row),
                  pl.BlockSpec((tm, GROUP_W), row),
                  _layer_spec(w_out, layer),
                  _layer_spec(ln_g, layer),
                  _layer_spec(ln_b, layer)],
        out_specs=pl.BlockSpec((tm, D_MODEL), row),
        out_shape=jax.ShapeDtypeStruct((m, D_MODEL), F32),
        compiler_params=_params(("parallel",)),
        name="out_proj_norm",
    )(x2, *groups, w_out, ln_g, ln_b)


def _split_w_in(w_in):
    half = RET_DK // 2
    perm = jnp.array([h * RET_DK + s * half + j for s in range(2) for h in range(HEADS)
                      for j in range(half)], jnp.int32)
    n_qk = HEADS * RET_DK
    w_qk = jnp.concatenate([w_in[:, :n_qk][:, perm],
                            w_in[:, n_qk:2 * n_qk][:, perm] * RET_DK ** -0.5],
                           axis=1).astype(BF16)
    w_xa = w_in[:, C_AQ:C_AQ + W_XA].astype(BF16)
    w_if = jnp.pad(w_in[:, C_MI:C_AQ], ((0, 0), (0, W_IF - 2 * HEADS))).astype(BF16)
    return w_in[:, :C_MI].astype(BF16), w_qk, w_xa, w_if


def _prepare_weights(w_in, s5_a_re, s5_a_im, s5_log_dt, s5_b_re, s5_b_im, s5_c_re, s5_c_im,
                     s5_d, s5_w_glu, ml_conv_w, ml_wq, ml_wk, ml_b_ig, ml_b_fg, ml_gn, xa_w_kv, w_out,
                     ln_g, ln_b):
    bias = jnp.concatenate([ml_b_ig, ml_b_fg, jnp.zeros((W_IF - 2 * HEADS,), F32)])[None, :]
    return dict(
        w_in=_split_w_in(w_in),
        s5=_s5_weights(s5_a_re, s5_a_im, s5_log_dt, s5_b_re, s5_b_im, s5_c_re, s5_c_im, s5_d,
                       s5_w_glu),
        ml=(ml_conv_w, ml_wq.astype(BF16), jnp.swapaxes(ml_wk, 1, 2).astype(BF16), bias,
            ml_gn[None, :]),
        w_kv=xa_w_kv.astype(BF16), w_out=w_out.astype(BF16), ln_g=ln_g[None, :], ln_b=ln_b[None, :])


def _layer(x2, mem2, cos, sin, ret_tabs, bsz, seq, w, layer):
    p_ml, p_s5, p_xa, p_ret, p_if = _project(x2, w["w_in"], layer, bsz, seq)
    s5o = _s5(p_s5, w["s5"], layer, bsz, seq)
    mk, mv = _memory_kv(mem2, w["w_kv"], layer)
    ret, ml, xa = _mixers(p_ret, cos, sin, ret_tabs, p_ml, p_if, w["ml"], layer, p_xa, mk, mv,
                          bsz, seq)
    return _out_project(x2, (ret, s5o, ml, xa), w["w_out"], w["ln_g"], w["ln_b"], layer, seq)


def kernel(x, mem, positions, w_in, s5_a_re, s5_a_im, s5_log_dt, s5_b_re, s5_b_im, s5_c_re, s5_c_im,
           s5_d, s5_w_glu, ml_conv_w, ml_wq, ml_wk, ml_b_ig, ml_b_fg, ml_gn, xa_w_kv, w_out, ln_g,
           ln_b):
    bsz, seq, _ = x.shape
    x2 = x.reshape(bsz * seq, D_MODEL)
    mem2 = mem.reshape(bsz * N_MEM, D_MODEL)
    cos, sin = _rope_tables(positions)
    ret_tabs = _retention_tables()
    weights = jax.vmap(_prepare_weights)(
        w_in, s5_a_re, s5_a_im, s5_log_dt, s5_b_re, s5_b_im, s5_c_re, s5_c_im, s5_d, s5_w_glu,
        ml_conv_w, ml_wq, ml_wk, ml_b_ig, ml_b_fg, ml_gn, xa_w_kv, w_out, ln_g, ln_b)
    for l in range(w_in.shape[0]):
        x2 = _layer(x2, mem2, cos, sin, ret_tabs, bsz, seq, weights, l)
    return x2.reshape(bsz, seq, D_MODEL)
```
